```python
import jax, jax.numpy as jnp
from jax import lax
import numpy as np

D_MODEL = 2048
BATCH = 4
SEQ = 2048
DEPTH = 1
DEC_BATCH = 32
DEC_SEQ = 64
PAST_LEN = 2048

CHUNK = 64
M_HEADS = 8
M_DK = 128
M_DV = 128
F_BIAS = 3.0
A_HEADS = 8
A_KV = 2
A_DH = 128
I_HEADS = 8
I_DH = 64
TOPK_MAX = 256
Q_BLOCK = 128
N_MEM = 256
C_HEADS = 4
C_DH = 256
N_BRANCH = 3
BRANCH_W = 1024
D_FF = 5632
CONV_W = 3
ROPE_THETA = 500000.0
ROPE_FRAC = 4
EPS = 1e-6
NEG = -1e30

IN_SPLITS = (M_HEADS * M_DK, M_HEADS * M_DK, M_HEADS * M_DV, M_HEADS * M_DV, M_HEADS, M_HEADS,
             A_HEADS * A_DH, A_KV * A_DH, A_KV * A_DH, I_HEADS * I_DH, I_DH, I_HEADS,
             C_HEADS * C_DH, N_BRANCH * D_MODEL)
IN_WIDTH = sum(IN_SPLITS)

kernel_name = "hybrid_mlstm_dsa_memxattn_convffn_step"


def rmsnorm(x, g):
    x32 = x.astype(jnp.float32)
    y = x32 * lax.rsqrt(jnp.mean(x32 * x32, axis=-1, keepdims=True) + EPS)
    return (y * g.astype(jnp.float32)).astype(x.dtype)


def partial_rope(x, pos):
    d = x.shape[-1]
    r = d // ROPE_FRAC
    half = r // 2
    inv = ROPE_THETA ** (-jnp.arange(half, dtype=jnp.float32) * 2.0 / r)
    ang = pos.astype(jnp.float32)[:, None] * inv[None, :]
    cos = jnp.cos(ang)[None, :, None, :]
    sin = jnp.sin(ang)[None, :, None, :]
    x32 = x.astype(jnp.float32)
    x1 = x32[..., :half]
    x2 = x32[..., half:r]
    out = jnp.concatenate([x1 * cos - x2 * sin, x2 * cos + x1 * sin, x32[..., r:]], axis=-1)
    return out.astype(x.dtype)


def split_in(p):
    outs = []
    off = 0
    for w in IN_SPLITS:
        outs.append(p[..., off:off + w])
        off += w
    return outs


def mlstm_chunk(carry, xs):
    C, n, m = carry
    q, k, v, ig, lf = xs
    L = q.shape[1]
    b = jnp.cumsum(lf, axis=1).transpose(0, 2, 1)
    igT = ig.transpose(0, 2, 1)
    causal = jnp.tril(jnp.ones((L, L), dtype=bool))
    logD = jnp.where(causal, b[..., :, None] - b[..., None, :] + igT[..., None, :], -jnp.inf)
    m_inter = b + m[..., None]
    m_t = jnp.maximum(m_inter, jnp.max(logD, axis=-1))
    D = jnp.exp(logD - m_t[..., None])
    inter = jnp.exp(m_inter - m_t)
    S = jnp.einsum('bthd,bshd->bhts', q, k) * D
    num = (jnp.einsum('bhts,bshv->bthv', S, v)
           + jnp.einsum('bhvd,bthd->bthv', C, q) * inter.transpose(0, 2, 1)[..., None])
    den = jnp.sum(S, axis=-1) + inter * jnp.einsum('bhd,bthd->bht', n, q)
    h = num / jnp.maximum(jnp.abs(den), jnp.exp(-m_t)).transpose(0, 2, 1)[..., None]
    m_new = m_t[..., -1]
    w_s = jnp.exp(b[..., -1:] - b + igT - m_new[..., None])
    decay = jnp.exp(b[..., -1] + m - m_new)
    C_new = decay[..., None, None] * C + jnp.einsum('bhs,bshv,bshd->bhvd', w_s, v, k)
    n_new = decay[..., None] * n + jnp.einsum('bhs,bshd->bhd', w_s, k)
    return (C_new, n_new, m_new), h


def mlstm_scan(q, k, v, ig, lf, state):
    B, T = q.shape[:2]
    L = min(T, CHUNK)
    nc = T // L

    def to_chunks(a):
        return jnp.moveaxis(a.reshape((B, nc, L) + a.shape[2:]), 1, 0)

    state, h = lax.scan(mlstm_chunk, state, (to_chunks(q), to_chunks(k), to_chunks(v),
                                              to_chunks(ig), to_chunks(lf)))
    h = jnp.moveaxis(h, 0, 1).reshape(B, T, M_HEADS * M_DV)
    return h, state


def dsa_block(q, qi, wi, qpos, k_all, v_all, ki_all, topk):
    f32 = jnp.float32
    B, Q = q.shape[:2]
    S = k_all.shape[1]
    chunk_end = (qpos // CHUNK + 1) * CHUNK
    admissible = jnp.arange(S)[None, :] < chunk_end[:, None]
    logits = jnp.einsum('bqhd,bsd->bqhs', qi.astype(f32), ki_all.astype(f32)) * (I_DH ** -0.5)
    score = jnp.einsum('bqhs,bqh->bqs', jax.nn.relu(logits), wi.astype(f32)) * (I_HEADS ** -0.5)
    score = jnp.where(admissible[None], score, NEG)
    _, idx = lax.top_k(score, topk)
    valid = idx < chunk_end[None, :, None]
    gather = jax.vmap(lambda a, i: a[i])
    k_sel = gather(k_all, idx)
    v_sel = gather(v_all, idx)
    qg = q.reshape(B, Q, A_KV, A_HEADS // A_KV, A_DH)
    s = jnp.einsum('bqjgd,bqnjd->bqjgn', qg, k_sel).astype(f32) * (A_DH ** -0.5)
    s = jnp.where(valid[:, :, None, None, :], s, NEG)
    p = jax.nn.softmax(s, axis=-1).astype(v_sel.dtype)
    o = jnp.einsum('bqjgn,bqnjd->bqjgd', p, v_sel)
    return o.reshape(B, Q, A_HEADS * A_DH)


def dsa_attention(q, qi, wi, pos, k_all, v_all, ki_all, topk):
    B, T = q.shape[:2]
    qb = min(T, Q_BLOCK)
    nb = T // qb

    def blocks(a):
        return jnp.moveaxis(a.reshape((B, nb, qb) + a.shape[2:]), 1, 0)

    def one(args):
        bq, bqi, bwi, bpos = args
        return dsa_block(bq, bqi, bwi, bpos, k_all, v_all, ki_all, topk)

    out = lax.map(one, (blocks(q), blocks(qi), blocks(wi), pos.reshape(nb, qb)))
    return jnp.moveaxis(out, 0, 1).reshape(B, T, A_HEADS * A_DH)


def mem_attend(cq, mem_k, mem_v):
    B, T = cq.shape[:2]
    s = jnp.einsum('bthd,bmhd->bhtm', cq, mem_k).astype(jnp.float32) * (C_DH ** -0.5)
    p = jax.nn.softmax(s, axis=-1).astype(mem_v.dtype)
    return jnp.einsum('bhtm,bmhd->bthd', p, mem_v).reshape(B, T, C_HEADS * C_DH)


def mixer(x, pos, past, mstate, mem_k, mem_v, topk, g_mix, w_in, b_ig, b_fg, w_branch, w_out):
    f32 = jnp.float32
    B, T, _ = x.shape
    xn = rmsnorm(x, g_mix)
    (mq, mk, mv, mo, mi, mf, aq, ak, av, iq, ik, iw, cq, gt) = split_in(xn @ w_in)
    q = mq.reshape(B, T, M_HEADS, M_DK).astype(f32)
    k = mk.reshape(B, T, M_HEADS, M_DK).astype(f32) * (M_DK ** -0.5)
    v = mv.reshape(B, T, M_HEADS, M_DV).astype(f32)
    ig = mi.astype(f32) + b_ig.astype(f32)
    lf = jax.nn.log_sigmoid(mf.astype(f32) + b_fg.astype(f32))
    h, new_mstate = mlstm_scan(q, k, v, ig, lf, mstate)
    branch_a = (h * jax.nn.sigmoid(mo.astype(f32))).astype(x.dtype)
    qa = partial_rope(aq.reshape(B, T, A_HEADS, A_DH), pos)
    k_new = partial_rope(ak.reshape(B, T, A_KV, A_DH), pos)
    v_new = av.reshape(B, T, A_KV, A_DH)
    qi = partial_rope(iq.reshape(B, T, I_HEADS, I_DH), pos)
    ki_new = partial_rope(ik.reshape(B, T, 1, I_DH), pos)[:, :, 0]
    if past is None:
        k_all, v_all, ki_all = k_new, v_new, ki_new
    else:
        pk, pv, pki = past
        k_all = jnp.concatenate([pk, k_new.astype(pk.dtype)], axis=1)
        v_all = jnp.concatenate([pv, v_new.astype(pv.dtype)], axis=1)
        ki_all = jnp.concatenate([pki, ki_new.astype(pki.dtype)], axis=1)
    branch_b = dsa_attention(qa, qi, iw, pos, k_all, v_all, ki_all, topk).astype(x.dtype)
    branch_c = mem_attend(cq.reshape(B, T, C_HEADS, C_DH), mem_k, mem_v).astype(x.dtype)
    branches = jnp.stack([branch_a, branch_b, branch_c], axis=2)
    proj = jnp.einsum('btnc,ncd->btnd', branches, w_branch)
    gates = jax.nn.sigmoid(gt.reshape(B, T, N_BRANCH, D_MODEL))
    merged = jnp.sum(gates * proj, axis=2)
    return merged @ w_out, (k_new, v_new, ki_new), new_mstate


def conv_ffn(x, conv_prev, g_ffn, w_up, conv_w, conv_b, w_down):
    T = x.shape[1]
    u = rmsnorm(x, g_ffn) @ w_up
    ext = jnp.concatenate([conv_prev.astype(u.dtype), u], axis=1)
    c = conv_b + ext[:, 0:T] * conv_w[0]
    for j in range(1, CONV_W):
        c = c + ext[:, j:j + T] * conv_w[j]
    gate, val = jnp.split(c, 2, axis=-1)
    y = (jax.nn.silu(gate) * val) @ w_down
    return y, ext[:, -(CONV_W - 1):]


def setup_inputs(seed: int = 0) -> dict:
    key = jax.random.key(seed)
    ks = jax.random.split(key, 28)
    f32 = jnp.float32

    def nrm(k, shape, scale=1.0):
        return jax.random.normal(k, shape, f32) * scale

    F2 = 2 * D_FF
    return {
        "x_prompt": nrm(ks[0], (BATCH, SEQ, D_MODEL)),
        "x_sample": nrm(ks[1], (DEC_BATCH, DEC_SEQ, D_MODEL)),
        "mem_prompt": nrm(ks[2], (BATCH, N_MEM, D_MODEL)),
        "cache_dsa_k": nrm(ks[3], (DEPTH, DEC_BATCH, PAST_LEN, A_KV, A_DH)),
        "cache_dsa_v": nrm(ks[4], (DEPTH, DEC_BATCH, PAST_LEN, A_KV, A_DH)),
        "cache_dsa_kidx": nrm(ks[5], (DEPTH, DEC_BATCH, PAST_LEN, I_DH)),
        "state_mlstm_C": nrm(ks[6], (DEPTH, DEC_BATCH, M_HEADS, M_DV, M_DK), M_DK ** -0.5),
        "state_mlstm_n": nrm(ks[7], (DEPTH, DEC_BATCH, M_HEADS, M_DK), M_DK ** -0.5),
        "state_mlstm_m": nrm(ks[8], (DEPTH, DEC_BATCH, M_HEADS)),
        "state_ffn_conv": nrm(ks[9], (DEPTH, DEC_BATCH, CONV_W - 1, F2)),
        "cache_mem_k": nrm(ks[10], (DEPTH, DEC_BATCH, N_MEM, C_HEADS, C_DH)),
        "cache_mem_v": nrm(ks[11], (DEPTH, DEC_BATCH, N_MEM, C_HEADS, C_DH)),
        "g_mix": 1.0 + nrm(ks[12], (DEPTH, D_MODEL), 0.05),
        "w_in": nrm(ks[13], (DEPTH, D_MODEL, IN_WIDTH), D_MODEL ** -0.5),
        "b_igate": nrm(ks[14], (DEPTH, M_HEADS), 0.1),
        "b_fgate": F_BIAS + nrm(ks[15], (DEPTH, M_HEADS), 0.5),
        "w_branch": nrm(ks[16], (DEPTH, N_BRANCH, BRANCH_W, D_MODEL), BRANCH_W ** -0.5),
        "w_out": nrm(ks[17], (DEPTH, D_MODEL, D_MODEL), D_MODEL ** -0.5),
        "g_mem": 1.0 + nrm(ks[18], (DEPTH, D_MODEL), 0.05),
        "w_mem_kv": nrm(ks[19], (DEPTH, D_MODEL, 2 * C_HEADS * C_DH), D_MODEL ** -0.5),
        "g_ffn": 1.0 + nrm(ks[20], (DEPTH, D_MODEL), 0.05),
        "w_up": nrm(ks[21], (DEPTH, D_MODEL, F2), D_MODEL ** -0.5),
        "conv_w": nrm(ks[22], (DEPTH, CONV_W, F2), CONV_W ** -0.5),
        "conv_b": nrm(ks[23], (DEPTH, F2), 0.02),
        "w_down": nrm(ks[24], (DEPTH, D_FF, D_MODEL), D_FF ** -0.5),
        "g_final": 1.0 + nrm(ks[25], (D_MODEL,), 0.05),
    }


def reference(x_prompt, x_sample, mem_prompt, cache_dsa_k, cache_dsa_v, cache_dsa_kidx,
              state_mlstm_C, state_mlstm_n, state_mlstm_m, state_ffn_conv, cache_mem_k, cache_mem_v,
              g_mix, w_in, b_igate, b_fgate, w_branch, w_out, g_mem, w_mem_kv, g_ffn, w_up,
              conv_w, conv_b, w_down, g_final):
    f32 = jnp.float32
    B, T = x_prompt.shape[:2]
    DB, DT = x_sample.shape[:2]
    P = cache_dsa_k.shape[2]
    pos_p = jnp.arange(T, dtype=jnp.int32)
    pos_s = P + jnp.arange(DT, dtype=jnp.int32)
    topk_p = min(TOPK_MAX, T // 4)
    topk_s = min(TOPK_MAX, (P + DT) // 4)
    xp, xs = x_prompt, x_sample
    pk, pv, pki, pC, pn, pm, pconv, pmk, pmv = [], [], [], [], [], [], [], [], []
    sk, sv, ski, sC, sn, sm, sconv = [], [], [], [], [], [], []
    for l in range(DEPTH):
        lw = (g_mix[l], w_in[l], b_igate[l], b_fgate[l], w_branch[l], w_out[l])
        mkv = rmsnorm(mem_prompt, g_mem[l]) @ w_mem_kv[l]
        mk_p = mkv[..., :C_HEADS * C_DH].reshape(B, N_MEM, C_HEADS, C_DH)
        mv_p = mkv[..., C_HEADS * C_DH:].reshape(B, N_MEM, C_HEADS, C_DH)
        zero_state = (jnp.zeros((B, M_HEADS, M_DV, M_DK), f32), jnp.zeros((B, M_HEADS, M_DK), f32),
                      jnp.zeros((B, M_HEADS), f32))
        a, (k_p, v_p, ki_p), (C_p, n_p, m_p) = mixer(xp, pos_p, None, zero_state, mk_p, mv_p, topk_p, *lw)
        xp = xp + a
        zero_conv = jnp.zeros((B, CONV_W - 1, 2 * D_FF), xp.dtype)
        f, conv_p = conv_ffn(xp, zero_conv, g_ffn[l], w_up[l], conv_w[l], conv_b[l], w_down[l])
        xp = xp + f
        pk.append(k_p); pv.append(v_p); pki.append(ki_p)
        pC.append(C_p.astype(x_prompt.dtype)); pn.append(n_p.astype(x_prompt.dtype)); pm.append(m_p.astype(x_prompt.dtype))
        pconv.append(conv_p); pmk.append(mk_p); pmv.append(mv_p)
        mstate = (state_mlstm_C[l].astype(f32), state_mlstm_n[l].astype(f32), state_mlstm_m[l].astype(f32))
        past = (cache_dsa_k[l], cache_dsa_v[l], cache_dsa_kidx[l])
        a, (k_s, v_s, ki_s), (C_s, n_s, m_s) = mixer(xs, pos_s, past, mstate, cache_mem_k[l], cache_mem_v[l], topk_s, *lw)
        xs = xs + a
        f, conv_s = conv_ffn(xs, state_ffn_conv[l], g_ffn[l], w_up[l], conv_w[l], conv_b[l], w_down[l])
        xs = xs + f
        sk.append(k_s); sv.append(v_s); ski.append(ki_s)
        sC.append(C_s.astype(x_sample.dtype)); sn.append(n_s.astype(x_sample.dtype)); sm.append(m_s.astype(x_sample.dtype))
        sconv.append(conv_s)
    y_prompt = rmsnorm(xp, g_final)
    y_sample = rmsnorm(xs, g_final)
    return (y_prompt, y_sample,
            jnp.stack(pk), jnp.stack(pv), jnp.stack(pki), jnp.stack(pC), jnp.stack(pn), jnp.stack(pm),
            jnp.stack(pconv), jnp.stack(pmk), jnp.stack(pmv),
            jnp.stack(sk), jnp.stack(sv), jnp.stack(ski), jnp.stack(sC), jnp.stack(sn), jnp.stack(sm),
            jnp.stack(sconv))
```

```python
import functools
import math

import jax
import jax.numpy as jnp
from jax import lax
from jax.experimental import pallas as pl
from jax.experimental.pallas import tpu as pltpu

F32 = jnp.float32
BF16 = jnp.bfloat16

D_MODEL = 2048
CHUNK = 64
M_HEADS, M_DK, M_DV = 8, 128, 128
A_HEADS, A_KV, A_DH = 8, 2, 128
I_HEADS, I_DH = 8, 64
TOPK_MAX = 256
N_MEM, C_HEADS, C_DH = 256, 4, 256
N_BRANCH, BRANCH_W = 3, 1024
D_FF = 5632
CONV_W = 3
ROPE_THETA = 500000.0
ROPE_FRAC = 4
EPS = 1e-6
NEG = -1e30

LANES = 128
SUBLANES = 8
VMEM_LIMIT = 48 * 1024 * 1024

ORIG_SPLITS = (1024, 1024, 1024, 1024, 8, 8, 1024, 256, 256, 512, 64, 8, 1024, 6144)
ORIG_NAMES = ("mq", "mk", "mv", "mo", "mi", "mf", "aq", "ak", "av", "iq", "ik", "iw", "cq", "gt")
MAIN_ORDER = ("mq", "mk", "mv", "mo", "aq", "cq", "gt", "ak", "av", "iq")
SMALL_ORDER = ("ik", "mi", "mf", "iw")
MAIN_W = 13312
COL = {}
_off = 0
for _n in MAIN_ORDER:
    COL[_n] = _off
    _off += ORIG_SPLITS[ORIG_NAMES.index(_n)]
assert _off == MAIN_W
SCOL = {"ik": 0, "mi": 64, "mf": 72, "iw": 80}


def _cparams(sem):
    return pltpu.CompilerParams(dimension_semantics=sem, vmem_limit_bytes=VMEM_LIMIT)


def _dot(a, b):
    return jnp.dot(a, b, preferred_element_type=F32)


def _dot_nt(a, b):
    return lax.dot_general(a, b, (((1,), (1,)), ((), ())), preferred_element_type=F32)


def _dot_tn(a, b):
    return lax.dot_general(a, b, (((0,), (0,)), ((), ())), preferred_element_type=F32)


def _norm_matmul_kernel(x_ref, g_ref, w_ref, o_ref, xn_ref):
    @pl.when(pl.program_id(1) == 0)
    def _():
        x = x_ref[...]
        ms = jnp.mean(x * x, axis=-1, keepdims=True)
        xn_ref[...] = (x * lax.rsqrt(ms + EPS) * g_ref[...]).astype(BF16)

    o_ref[...] = _dot(xn_ref[...], w_ref[...]).astype(o_ref.dtype)


def norm_matmul(x, g, w, tm, tn, name):
    m, k = x.shape
    n = w.shape[1]
    return pl.pallas_call(
        _norm_matmul_kernel,
        out_shape=jax.ShapeDtypeStruct((m, n), F32),
        grid=(m // tm, n // tn),
        in_specs=[pl.BlockSpec((tm, k), lambda i, j: (i, 0)),
                  pl.BlockSpec((1, k), lambda i, j: (0, 0)),
                  pl.BlockSpec((k, tn), lambda i, j: (0, j))],
        out_specs=pl.BlockSpec((tm, tn), lambda i, j: (i, j)),
        scratch_shapes=[pltpu.VMEM((tm, k), BF16)],
        compiler_params=_cparams(("arbitrary", "arbitrary")),
        name=name,
    )(x, g.reshape(1, k), w)


def _matmul_res_kernel(a_ref, w_ref, r_ref, o_ref):
    o_ref[...] = _dot(a_ref[...], w_ref[...]) + r_ref[...]


def matmul_res(a, w, res, tm, tn, name):
    m, k = a.shape
    n = w.shape[1]
    return pl.pallas_call(
        _matmul_res_kernel,
        out_shape=jax.ShapeDtypeStruct((m, n), F32),
        grid=(m // tm, n // tn),
        in_specs=[pl.BlockSpec((tm, k), lambda i, j: (i, 0)),
                  pl.BlockSpec((k, tn), lambda i, j: (0, j)),
                  pl.BlockSpec((tm, tn), lambda i, j: (i, j))],
        out_specs=pl.BlockSpec((tm, tn), lambda i, j: (i, j)),
        compiler_params=_cparams(("arbitrary", "arbitrary")),
        name=name,
    )(a, w, res)


def _rope_tables(pos, dh):
    r = dh // ROPE_FRAC
    half = r // 2
    inv = ROPE_THETA ** (-jnp.arange(half, dtype=F32) * 2.0 / r)
    ang = pos.astype(F32)[:, None] * inv[None, :]
    cos, sin = jnp.cos(ang), jnp.sin(ang)
    t = pos.shape[0]
    ones = jnp.ones((t, dh - r), F32)
    zeros_h = jnp.zeros((t, half), F32)
    zeros_rest = jnp.zeros((t, dh - r), F32)
    c = jnp.concatenate([cos, cos, ones], axis=1)
    sa = jnp.concatenate([zeros_h, sin, zeros_rest], axis=1)
    sb = jnp.concatenate([-sin, zeros_h, zeros_rest], axis=1)
    rep = LANES // dh
    return tuple(jnp.tile(a, (1, rep)) for a in (c, sa, sb))


def _rope_apply(x, c, sa, sb, half):
    return x * c + pltpu.roll(x, half, 1) * sa + pltpu.roll(x, LANES - half, 1) * sb


def _rope_kernel(aq_ref, ak_ref, iq_ref, sm_ref, c128_ref, sa128_ref, sb128_ref, c64_ref, sa64_ref, sb64_ref,
                 q_ref, qi_ref, k_ref, ki_ref):
    c1, a1, b1 = c128_ref[...], sa128_ref[...], sb128_ref[...]
    c2, a2, b2 = c64_ref[...], sa64_ref[...], sb64_ref[...]
    h1 = A_DH // ROPE_FRAC // 2
    h2 = I_DH // ROPE_FRAC // 2
    for h in range(A_HEADS):
        sl = slice(h * LANES, (h + 1) * LANES)
        q_ref[:, sl] = _rope_apply(aq_ref[:, sl], c1, a1, b1, h1).astype(q_ref.dtype)
    for h in range(A_KV):
        sl = slice(h * LANES, (h + 1) * LANES)
        k_ref[:, sl] = _rope_apply(ak_ref[:, sl], c1, a1, b1, h1)
    for h in range(I_HEADS * I_DH // LANES):
        sl = slice(h * LANES, (h + 1) * LANES)
        qi_ref[:, sl] = _rope_apply(iq_ref[:, sl], c2, a2, b2, h2).astype(qi_ref.dtype)
    ki = _rope_apply(sm_ref[...], c2, a2, b2, h2)
    ki_ref[...] = ki[:, :I_DH]


def rope(p_main, p_small, pos, seq, tr):
    m = p_main.shape[0]
    t128 = _rope_tables(pos, A_DH)
    t64 = _rope_tables(pos, I_DH)
    if seq < tr:
        t128 = tuple(jnp.tile(a, (tr // seq, 1)) for a in t128)
        t64 = tuple(jnp.tile(a, (tr // seq, 1)) for a in t64)
    ntab = max(seq, tr) // tr
    tab_spec = pl.BlockSpec((tr, LANES), lambda i: (i % ntab, 0))
    return pl.pallas_call(
        _rope_kernel,
        out_shape=(jax.ShapeDtypeStruct((m, A_HEADS * A_DH), BF16),
                   jax.ShapeDtypeStruct((m, I_HEADS * I_DH), BF16),
                   jax.ShapeDtypeStruct((m, A_KV * A_DH), F32),
                   jax.ShapeDtypeStruct((m, I_DH), F32)),
        grid=(m // tr,),
        in_specs=[pl.BlockSpec((tr, 1024), lambda i: (i, COL["aq"] // 1024)),
                  pl.BlockSpec((tr, 256), lambda i: (i, COL["ak"] // 256)),
                  pl.BlockSpec((tr, 512), lambda i: (i, COL["iq"] // 512)),
                  pl.BlockSpec((tr, LANES), lambda i: (i, 0)),
                  tab_spec, tab_spec, tab_spec, tab_spec, tab_spec, tab_spec],
        out_specs=(pl.BlockSpec((tr, 1024), lambda i: (i, 0)),
                   pl.BlockSpec((tr, 512), lambda i: (i, 0)),
                   pl.BlockSpec((tr, 256), lambda i: (i, 0)),
                   pl.BlockSpec((tr, I_DH), lambda i: (i, 0))),
        compiler_params=_cparams(("arbitrary",)),
        name="rope",
    )(p_main, p_main, p_main, p_small, *t128, *t64)


def _split3(x):
    hi = x.astype(BF16)
    r1 = x - hi.astype(F32)
    mid = r1.astype(BF16)
    lo = (r1 - mid.astype(F32)).astype(BF16)
    return hi, mid, lo


def _log_sigmoid(x):
    return jnp.minimum(x, 0.0) - jnp.log1p(jnp.exp(-jnp.abs(x)))


def _mlstm_kernel(q_ref, k_ref, v_ref, o_ref, sm_ref, big_r_ref, bfg_r_ref,
                  c0_ref, n0_ref, m0_ref, h_ref, c_ref, n_ref, m_ref):
    L = CHUNK

    @pl.when(pl.program_id(1) == 0)
    def _():
        c_ref[...] = c0_ref[...]
        n_ref[...] = n0_ref[...]
        m_ref[...] = m0_ref[...]

    g = sm_ref[...]
    ig_c = g[:, SCOL["mi"]:SCOL["mi"] + M_HEADS] + big_r_ref[...]
    lf_c = _log_sigmoid(g[:, SCOL["mf"]:SCOL["mf"] + M_HEADS] + bfg_r_ref[...])

    row = lax.broadcasted_iota(jnp.int32, (L, L), 0)
    col = lax.broadcasted_iota(jnp.int32, (L, L), 1)
    causal = col <= row
    tril = jnp.where(causal, 1.0, 0.0).astype(BF16)
    triu = jnp.where(row <= col, 1.0, 0.0).astype(BF16)
    eye = jnp.where(row == col, 1.0, 0.0).astype(BF16)
    lf_parts = _split3(lf_c)
    b_c = sum(_dot(tril, part) for part in lf_parts)
    b_r = sum(_dot_tn(part, triu) for part in lf_parts)
    ig_r = sum(_dot_tn(part, eye) for part in _split3(ig_c))

    scale = M_DK ** -0.5
    for h in range(M_HEADS):
        sl = slice(h * LANES, (h + 1) * LANES)
        bc = b_c[:, h:h + 1]
        br = b_r[h:h + 1, :]
        igr = ig_r[h:h + 1, :]
        igc = ig_c[:, h:h + 1]
        m_prev = m_ref[0, h:h + 1, 0:1]
        c_prev = c_ref[0, h]
        n_prev = n_ref[0, h:h + 1, :]

        log_d = jnp.where(causal, bc - br + igr, -jnp.inf)
        m_inter = bc + m_prev
        m_t = jnp.maximum(m_inter, jnp.max(log_d, axis=-1, keepdims=True))
        d = jnp.exp(log_d - m_t)
        inter = jnp.exp(m_inter - m_t)

        qh = q_ref[:, sl]
        kh = k_ref[:, sl] * scale
        vh = v_ref[:, sl]
        qb = qh.astype(BF16)
        kb = kh.astype(BF16)
        s = _dot_nt(qb, kb) * d
        num = _dot(s.astype(BF16), vh.astype(BF16)) + _dot_nt(qb, c_prev.astype(BF16)) * inter
        den = jnp.sum(s, axis=-1, keepdims=True) + inter * jnp.sum(qh * n_prev, axis=-1, keepdims=True)
        hout = num / jnp.maximum(jnp.abs(den), jnp.exp(-m_t))
        gate = 1.0 / (1.0 + jnp.exp(-o_ref[:, sl]))
        h_ref[:, sl] = (hout * gate).astype(h_ref.dtype)

        m_new = m_t[L - 1:L, :]
        b_last = bc[L - 1:L, :]
        w_c = jnp.exp(b_last - bc + igc - m_new)
        decay = jnp.exp(b_last + m_prev - m_new)
        c_ref[0, h] = decay * c_prev + _dot_tn((vh * w_c).astype(BF16), kb)
        n_ref[0, h:h + 1, :] = decay * n_prev + jnp.sum(kh * w_c, axis=0, keepdims=True)
        m_ref[0, h:h + 1, :] = jnp.broadcast_to(m_new, (1, LANES))


def mlstm(p_main, p_small, b_ig, b_fg, c0, n0, m0, batch, seq):
    nc = seq // CHUNK
    m = batch * seq
    rowblk = lambda cb: pl.BlockSpec((CHUNK, 1024), lambda b, c: (b * nc + c, cb))
    st4 = pl.BlockSpec((1, M_HEADS, M_DV, M_DK), lambda b, c: (b, 0, 0, 0))
    st3 = pl.BlockSpec((1, M_HEADS, LANES), lambda b, c: (b, 0, 0))
    vec_r = pl.BlockSpec((1, M_HEADS), lambda b, c: (0, 0))
    return pl.pallas_call(
        _mlstm_kernel,
        out_shape=(jax.ShapeDtypeStruct((m, M_HEADS * M_DV), BF16),
                   jax.ShapeDtypeStruct((batch, M_HEADS, M_DV, M_DK), F32),
                   jax.ShapeDtypeStruct((batch, M_HEADS, M_DK), F32),
                   jax.ShapeDtypeStruct((batch, M_HEADS, LANES), F32)),
        grid=(batch, nc),
        in_specs=[rowblk(COL["mq"] // 1024), rowblk(COL["mk"] // 1024), rowblk(COL["mv"] // 1024),
                  rowblk(COL["mo"] // 1024),
                  pl.BlockSpec((CHUNK, LANES), lambda b, c: (b * nc + c, 0)),
                  vec_r, vec_r, st4, st3, st3],
        out_specs=(pl.BlockSpec((CHUNK, 1024), lambda b, c: (b * nc + c, 0)), st4, st3, st3),
        compiler_params=_cparams(("arbitrary", "arbitrary")),
        name="mlstm",
    )(p_main, p_main, p_main, p_main, p_small,
      b_ig.reshape(1, M_HEADS), b_fg.reshape(1, M_HEADS), c0, n0, m0)


def _count(mask):
    return jnp.sum(jnp.where(mask, 1.0, 0.0), axis=-1, keepdims=True)


def _dsa_kernel(*refs, tq, s_real, s_pad, s_past, seq, base_pos, topk):
    if s_past:
        (q_ref, qi_ref, sm_ref, kn_ref, vn_ref, kin_ref, pk_ref, pv_ref, pki_ref,
         out_ref, k_s, v_s, ki_s, sel_s) = refs
    else:
        (q_ref, qi_ref, sm_ref, kn_ref, vn_ref, kin_ref, out_ref, k_s, v_s, ki_s, sel_s) = refs
    qb = pl.program_id(1)

    @pl.when(qb == 0)
    def _():
        if s_past:
            k_s[0:s_past, :] = pk_ref[0].astype(BF16)
            v_s[0:s_past, :] = pv_ref[0].astype(BF16)
            ki_s[0:s_past, :] = pki_ref[0].astype(BF16)
        k_s[s_past:s_past + seq, :] = kn_ref[...].astype(BF16)
        v_s[s_past:s_past + seq, :] = vn_ref[...].astype(BF16)
        ki_s[s_past:s_past + seq, :] = kin_ref[...].astype(BF16)
        if s_pad > s_real:
            k_s[s_real:s_pad, :] = jnp.zeros((s_pad - s_real, A_KV * A_DH), BF16)
            v_s[s_real:s_pad, :] = jnp.zeros((s_pad - s_real, A_KV * A_DH), BF16)
            ki_s[s_real:s_pad, :] = jnp.zeros((s_pad - s_real, I_DH), BF16)

    qpos = base_pos + qb * tq + lax.broadcasted_iota(jnp.int32, (tq, 1), 0)
    chunk_end = (qpos // CHUNK + 1) * CHUNK
    sidx = lax.broadcasted_iota(jnp.int32, (1, s_pad), 1)
    admissible = sidx < chunk_end

    ki = ki_s[...]
    wi = sm_ref[:, SCOL["iw"]:SCOL["iw"] + I_HEADS]
    score = jnp.zeros((tq, s_pad), F32)
    for h in range(I_HEADS):
        lg = _dot_nt(qi_ref[:, h * I_DH:(h + 1) * I_DH], ki) * (I_DH ** -0.5)
        score = score + jnp.maximum(lg, 0.0) * wi[:, h:h + 1]
    score = score * (I_HEADS ** -0.5) + 0.0
    score = jnp.where(admissible, score, NEG)
    if s_pad > s_real:
        score = jnp.where(sidx < s_real, score, -jnp.inf)

    bits = pltpu.bitcast(score, jnp.int32)
    key = jnp.where(bits < 0, bits ^ jnp.int32(0x7FFFFFFF), bits)
    kf = float(topk)

    def thr_body(i, ans):
        cand = ans + lax.shift_left(jnp.int32(1), jnp.int32(31) - i)
        ok = _count(key >= cand) >= kf
        return jnp.where(ok, cand, ans)

    thr = lax.fori_loop(0, 32, thr_body, jnp.full((tq, 1), jnp.iinfo(jnp.int32).min, jnp.int32))
    gt_mask = key > thr
    eq_mask = key == thr
    need = kf - _count(gt_mask)
    n_eq = _count(eq_mask)
    sel_s[...] = jnp.where(key >= thr, 1.0, 0.0)

    @pl.when(jnp.max(n_eq - need) > 0.0)
    def _():
        def idx_body(i, j):
            cand = j + lax.shift_left(jnp.int32(1), jnp.int32(12) - i)
            ok = _count(eq_mask & (sidx < cand)) < need
            return jnp.where(ok, cand, j)

        jstar = lax.fori_loop(0, 13, idx_body, jnp.zeros((tq, 1), jnp.int32))
        sel_s[...] = jnp.where(gt_mask | (eq_mask & (sidx <= jstar)), 1.0, 0.0)

    mask = (sel_s[...] > 0.5) & admissible

    for j in range(A_KV):
        kj = k_s[:, j * A_DH:(j + 1) * A_DH]
        vj = v_s[:, j * A_DH:(j + 1) * A_DH]
        for g in range(A_HEADS // A_KV):
            h = j * (A_HEADS // A_KV) + g
            sl = slice(h * A_DH, (h + 1) * A_DH)
            s = _dot_nt(q_ref[:, sl], kj) * (A_DH ** -0.5)
            s = jnp.where(mask, s, NEG)
            mx = jnp.max(s, axis=-1, keepdims=True)
            p = jnp.exp(s - mx)
            l = jnp.sum(p, axis=-1, keepdims=True)
            o = _dot(p.astype(BF16), vj) / l
            out_ref[:, sl] = o.astype(out_ref.dtype)


def dsa(q_r, qi_r, p_small, k_new, v_src, v_colblk, ki_new, past, batch, seq, base_pos, tq):
    m = batch * seq
    nq = seq // tq
    s_past = 0 if past is None else past[0].shape[1]
    s_real = s_past + seq
    s_pad = -(-s_real // LANES) * LANES
    topk = min(TOPK_MAX, s_real // 4)
    assert topk <= s_real and s_pad < 8192
    qrow = lambda w: pl.BlockSpec((tq, w), lambda b, i: (b * nq + i, 0))
    in_specs = [qrow(1024), qrow(512), qrow(LANES),
                pl.BlockSpec((seq, 256), lambda b, i: (b, 0)),
                pl.BlockSpec((seq, 256), lambda b, i: (b, v_colblk)),
                pl.BlockSpec((seq, I_DH), lambda b, i: (b, 0))]
    args = [q_r, qi_r, p_small, k_new, v_src, ki_new]
    if past is not None:
        in_specs += [pl.BlockSpec((1, s_past, 256), lambda b, i: (b, 0, 0)),
                     pl.BlockSpec((1, s_past, 256), lambda b, i: (b, 0, 0)),
                     pl.BlockSpec((1, s_past, I_DH), lambda b, i: (b, 0, 0))]
        args += list(past)
    kern = functools.partial(_dsa_kernel, tq=tq, s_real=s_real, s_pad=s_pad, s_past=s_past, seq=seq,
                             base_pos=base_pos, topk=topk)
    return pl.pallas_call(
        kern,
        out_shape=jax.ShapeDtypeStruct((m, A_HEADS * A_DH), BF16),
        grid=(batch, nq),
        in_specs=in_specs,
        out_specs=pl.BlockSpec((tq, 1024), lambda b, i: (b * nq + i, 0)),
        scratch_shapes=[pltpu.VMEM((s_pad, A_KV * A_DH), BF16), pltpu.VMEM((s_pad, A_KV * A_DH), BF16),
                        pltpu.VMEM((s_pad, I_DH), BF16), pltpu.VMEM((tq, s_pad), F32)],
        compiler_params=_cparams(("arbitrary", "arbitrary")),
        name="dsa",
    )(*args)


def _mem_attn_kernel(q_ref, k_ref, v_ref, o_ref):
    for h in range(C_HEADS):
        sl = slice(h * C_DH, (h + 1) * C_DH)
        s = _dot_nt(q_ref[:, sl].astype(BF16), k_ref[:, sl].astype(BF16)) * (C_DH ** -0.5)
        mx = jnp.max(s, axis=-1, keepdims=True)
        p = jnp.exp(s - mx)
        l = jnp.sum(p, axis=-1, keepdims=True)
        o_ref[:, sl] = (_dot(p.astype(BF16), v_ref[:, sl].astype(BF16)) / l).astype(o_ref.dtype)


def mem_attn(p_main, mk, mk_colblk, mv, mv_colblk, batch, seq, tq):
    m = batch * seq
    nq = seq // tq
    w = C_HEADS * C_DH
    return pl.pallas_call(
        _mem_attn_kernel,
        out_shape=jax.ShapeDtypeStruct((m, w), BF16),
        grid=(batch, nq),
        in_specs=[pl.BlockSpec((tq, w), lambda b, i: (b * nq + i, COL["cq"] // w)),
                  pl.BlockSpec((N_MEM, w), lambda b, i: (b, mk_colblk)),
                  pl.BlockSpec((N_MEM, w), lambda b, i: (b, mv_colblk))],
        out_specs=pl.BlockSpec((tq, w), lambda b, i: (b * nq + i, 0)),
        compiler_params=_cparams(("arbitrary", "arbitrary")),
        name="mem_attn",
    )(p_main, mk, mv)


def _merge_kernel(a_ref, b_ref, c_ref, ga_ref, gb_ref, gc_ref, w_ref, o_ref):
    acc = None
    for br_ref, g_ref, n in ((a_ref, ga_ref, 0), (b_ref, gb_ref, 1), (c_ref, gc_ref, 2)):
        proj = _dot(br_ref[...], w_ref[n])
        term = proj / (1.0 + jnp.exp(-g_ref[...]))
        acc = term if acc is None else acc + term
    o_ref[...] = acc.astype(o_ref.dtype)


def merge(br_a, br_b, br_c, p_main, w_branch, tm, tn):
    m = br_a.shape[0]
    nj = D_MODEL // tn
    brs = pl.BlockSpec((tm, BRANCH_W), lambda i, j: (i, 0))
    gate = lambda n: pl.BlockSpec((tm, tn), lambda i, j: (i, (COL["gt"] + n * D_MODEL) // tn + j))
    return pl.pallas_call(
        _merge_kernel,
        out_shape=jax.ShapeDtypeStruct((m, D_MODEL), BF16),
        grid=(m // tm, nj),
        in_specs=[brs, brs, brs, gate(0), gate(1), gate(2),
                  pl.BlockSpec((N_BRANCH, BRANCH_W, tn), lambda i, j: (0, 0, j))],
        out_specs=pl.BlockSpec((tm, tn), lambda i, j: (i, j)),
        compiler_params=_cparams(("arbitrary", "arbitrary")),
        name="merge",
    )(br_a, br_b, br_c, p_main, p_main, p_main, w_branch)


def _ffn_up_kernel(h_ref, g_ref, wg_ref, wv_ref, cwg_ref, cwv_ref, cbg_ref, cbv_ref, pg_ref, pv_ref,
                   act_ref, tg_ref, tv_ref, xn_ref, ug_s, uv_s, cg_s, cv_s, *, tm, seg, blocks_per_seq):
    i = pl.program_id(0)
    j = pl.program_id(1)
    nseg = tm // seg
    hdr = SUBLANES

    @pl.when(j == 0)
    def _():
        x = h_ref[...]
        ms = jnp.mean(x * x, axis=-1, keepdims=True)
        xn_ref[...] = (x * lax.rsqrt(ms + EPS) * g_ref[...]).astype(BF16)

    xn = xn_ref[...]
    ug = _dot(xn, wg_ref[...])
    uv = _dot(xn, wv_ref[...])

    if blocks_per_seq > 1:
        @pl.when((i % blocks_per_seq) == 0)
        def _():
            cg_s[j] = pg_ref[0]
            cv_s[j] = pv_ref[0]

    def conv(useg, prev, cw_ref, cb_ref, u_s):
        u_s[hdr - 2:hdr, :] = prev
        u_s[hdr:hdr + seg, :] = useg
        return (cb_ref[...] + u_s[hdr - 2:hdr - 2 + seg, :] * cw_ref[0:1, :]
                + u_s[hdr - 1:hdr - 1 + seg, :] * cw_ref[1:2, :] + useg * cw_ref[2:3, :])

    for s in range(nseg):
        rows = slice(s * seg, (s + 1) * seg)
        ugs, uvs = ug[rows, :], uv[rows, :]
        prev_g = cg_s[j] if blocks_per_seq > 1 else pg_ref[s]
        prev_v = cv_s[j] if blocks_per_seq > 1 else pv_ref[s]
        cg = conv(ugs, prev_g, cwg_ref, cbg_ref, ug_s)
        cv = conv(uvs, prev_v, cwv_ref, cbv_ref, uv_s)
        act_ref[rows, :] = (cg / (1.0 + jnp.exp(-cg)) * cv).astype(act_ref.dtype)
        tail_g, tail_v = ugs[seg - 2:seg, :], uvs[seg - 2:seg, :]
        tg_ref[s] = tail_g
        tv_ref[s] = tail_v
        if blocks_per_seq > 1:
            cg_s[j] = tail_g
            cv_s[j] = tail_v


def ffn_up(h, g_ffn, w_up, conv_w, conv_b, conv_prev, seq, tm, tn):
    m = h.shape[0]
    seg = min(tm, seq)
    nseg = tm // seg
    bps = seq // seg
    blocks_per_seq = bps if nseg == 1 else 1
    nj = D_FF // tn
    prev_spec = lambda off: pl.BlockSpec((nseg, CONV_W - 1, tn),
                                         lambda i, j: ((i // blocks_per_seq) if nseg == 1 else i, 0, off + j))
    tail_spec = pl.BlockSpec((nseg, CONV_W - 1, tn), lambda i, j: (i, 0, j))
    kern = functools.partial(_ffn_up_kernel, tm=tm, seg=seg, blocks_per_seq=blocks_per_seq)
    return pl.pallas_call(
        kern,
        out_shape=(jax.ShapeDtypeStruct((m, D_FF), BF16),
                   jax.ShapeDtypeStruct((m // seg, CONV_W - 1, D_FF), F32),
                   jax.ShapeDtypeStruct((m // seg, CONV_W - 1, D_FF), F32)),
        grid=(m // tm, nj),
        in_specs=[pl.BlockSpec((tm, D_MODEL), lambda i, j: (i, 0)),
                  pl.BlockSpec((1, D_MODEL), lambda i, j: (0, 0)),
                  pl.BlockSpec((D_MODEL, tn), lambda i, j: (0, j)),
                  pl.BlockSpec((D_MODEL, tn), lambda i, j: (0, nj + j)),
                  pl.BlockSpec((CONV_W, tn), lambda i, j: (0, j)),
                  pl.BlockSpec((CONV_W, tn), lambda i, j: (0, nj + j)),
                  pl.BlockSpec((1, tn), lambda i, j: (0, j)),
                  pl.BlockSpec((1, tn), lambda i, j: (0, nj + j)),
                  prev_spec(0), prev_spec(nj)],
        out_specs=(pl.BlockSpec((tm, tn), lambda i, j: (i, j)), tail_spec, tail_spec),
        scratch_shapes=[pltpu.VMEM((tm, D_MODEL), BF16),
                        pltpu.VMEM((SUBLANES + seg, tn), F32), pltpu.VMEM((SUBLANES + seg, tn), F32),
                        pltpu.VMEM((nj, CONV_W - 1, tn), F32), pltpu.VMEM((nj, CONV_W - 1, tn), F32)],
        compiler_params=_cparams(("arbitrary", "arbitrary")),
        name="ffn_up",
    )(h, g_ffn.reshape(1, D_MODEL), w_up, w_up, conv_w, conv_w,
      conv_b.reshape(1, 2 * D_FF), conv_b.reshape(1, 2 * D_FF), conv_prev, conv_prev)


def _ffn_down_kernel(a_ref, w_ref, h_ref, g_ref, y_ref, acc_ref):
    k = pl.program_id(1)
    part = _dot(a_ref[...], w_ref[...])

    @pl.when(k == 0)
    def _():
        acc_ref[...] = part

    @pl.when(k > 0)
    def _():
        acc_ref[...] += part

    @pl.when(k == pl.num_programs(1) - 1)
    def _():
        z = h_ref[...] + acc_ref[...]
        ms = jnp.mean(z * z, axis=-1, keepdims=True)
        y_ref[...] = z * lax.rsqrt(ms + EPS) * g_ref[...]


def ffn_down(act, w_down, h, g_final, tm, tk):
    m = act.shape[0]
    return pl.pallas_call(
        _ffn_down_kernel,
        out_shape=jax.ShapeDtypeStruct((m, D_MODEL), F32),
        grid=(m // tm, D_FF // tk),
        in_specs=[pl.BlockSpec((tm, tk), lambda i, k: (i, k)),
                  pl.BlockSpec((tk, D_MODEL), lambda i, k: (k, 0)),
                  pl.BlockSpec((tm, D_MODEL), lambda i, k: (i, 0)),
                  pl.BlockSpec((1, D_MODEL), lambda i, k: (0, 0))],
        out_specs=pl.BlockSpec((tm, D_MODEL), lambda i, k: (i, 0)),
        scratch_shapes=[pltpu.VMEM((tm, D_MODEL), F32)],
        compiler_params=_cparams(("arbitrary", "arbitrary")),
        name="ffn_down",
    )(act, w_down, h, g_final.reshape(1, D_MODEL))


def _group(x, pos, past, mstate, mem_k, mem_kblk, mem_v, mem_vblk, conv_prev, wts, base_pos):
    (g_mix, w_main, w_small, b_ig, b_fg, w_branch, w_out, g_ffn, w_up, conv_w, conv_b, w_down, g_final) = wts
    batch, seq, _ = x.shape
    m = batch * seq
    x2 = x.reshape(m, D_MODEL)

    p_main = norm_matmul(x2, g_mix, w_main, 1024, 1024, "in_proj")
    p_small = norm_matmul(x2, g_mix, w_small, 1024, LANES, "in_proj_small")

    q_r, qi_r, k_new, ki_new = rope(p_main, p_small, pos, seq, 512)
    v_new = p_main[:, COL["av"]:COL["av"] + A_KV * A_DH]

    c0, n0, m0 = mstate
    br_a, c_new, n_new, m_new = mlstm(p_main, p_small, b_ig, b_fg, c0, n0,
                                      jnp.broadcast_to(m0[:, :, None], (batch, M_HEADS, LANES)), batch, seq)
    br_b = dsa(q_r, qi_r, p_small, k_new, p_main, COL["av"] // 256, ki_new, past, batch, seq, base_pos,
               min(seq, 128))
    br_c = mem_attn(p_main, mem_k, mem_kblk, mem_v, mem_vblk, batch, seq, min(seq, 512))

    merged = merge(br_a, br_b, br_c, p_main, w_branch, 1024, 512)
    h = matmul_res(merged, w_out, x2, 1024, 1024, "out_proj")

    act, tail_g, tail_v = ffn_up(h, g_ffn, w_up, conv_w, conv_b, conv_prev, seq, 1024, 512)
    y = ffn_down(act, w_down, h, g_final, 512, 512)

    nblk = tail_g.shape[0] // batch
    conv_new = jnp.concatenate([tail_g.reshape(batch, nblk, CONV_W - 1, D_FF)[:, -1],
                                tail_v.reshape(batch, nblk, CONV_W - 1, D_FF)[:, -1]], axis=-1)
    outs = (y.reshape(batch, seq, D_MODEL),
            k_new.reshape(batch, seq, A_KV, A_DH), v_new.reshape(batch, seq, A_KV, A_DH),
            ki_new.reshape(batch, seq, I_DH), c_new, n_new, m_new[:, :, 0], conv_new)
    return outs


def kernel(x_prompt, x_sample, mem_prompt, cache_dsa_k, cache_dsa_v, cache_dsa_kidx, state_mlstm_C, state_mlstm_n, state_mlstm_m, state_ffn_conv, cache_mem_k, cache_mem_v, g_mix, w_in, b_igate, b_fgate, w_branch, w_out, g_mem, w_mem_kv, g_ffn, w_up, conv_w, conv_b, w_down, g_final):
    depth = w_in.shape[0]
    assert depth == 1
    B, T = x_prompt.shape[:2]
    DB, DT = x_sample.shape[:2]
    P = cache_dsa_k.shape[2]
    l = 0

    w = w_in[l]
    segs = {}
    off = 0
    for name, width in zip(ORIG_NAMES, ORIG_SPLITS):
        segs[name] = w[:, off:off + width]
        off += width
    w_main = jnp.concatenate([segs[n] for n in MAIN_ORDER], axis=1).astype(BF16)
    small_w = sum(ORIG_SPLITS[ORIG_NAMES.index(n)] for n in SMALL_ORDER)
    w_small = jnp.concatenate([segs[n] for n in SMALL_ORDER] + [jnp.zeros((D_MODEL, LANES - small_w), F32)],
                              axis=1).astype(BF16)
    wts = (g_mix[l], w_main, w_small, b_igate[l], b_fgate[l], w_branch[l].astype(BF16), w_out[l].astype(BF16),
           g_ffn[l], w_up[l].astype(BF16), conv_w[l], conv_b[l], w_down[l].astype(BF16), g_final)

    mkv = norm_matmul(mem_prompt.reshape(B * N_MEM, D_MODEL), g_mem[l], w_mem_kv[l].astype(BF16), 512, 1024, "mem_kv")
    zero_state = (jnp.zeros((B, M_HEADS, M_DV, M_DK), F32), jnp.zeros((B, M_HEADS, M_DK), F32),
                  jnp.zeros((B, M_HEADS), F32))
    zero_conv = jnp.zeros((B, CONV_W - 1, 2 * D_FF), F32)
    (y_p, k_p, v_p, ki_p, c_p, n_p, m_p, conv_p) = _group(
        x_prompt, jnp.arange(T, dtype=jnp.int32), None, zero_state, mkv, 0, mkv, 1, zero_conv, wts, 0)
    hw = C_HEADS * C_DH
    mk_p = mkv[:, :hw].reshape(B, N_MEM, C_HEADS, C_DH)
    mv_p = mkv[:, hw:].reshape(B, N_MEM, C_HEADS, C_DH)

    past = (cache_dsa_k[l].reshape(DB, P, A_KV * A_DH), cache_dsa_v[l].reshape(DB, P, A_KV * A_DH),
            cache_dsa_kidx[l])
    mstate = (state_mlstm_C[l], state_mlstm_n[l], state_mlstm_m[l])
    (y_s, k_s, v_s, ki_s, c_s, n_s, m_s, conv_s) = _group(
        x_sample, P + jnp.arange(DT, dtype=jnp.int32), past, mstate,
        cache_mem_k[l].reshape(DB * N_MEM, hw), 0, cache_mem_v[l].reshape(DB * N_MEM, hw), 0,
        state_ffn_conv[l], wts, P)

    st = lambda a: a[None]
    return (y_p, y_s,
            st(k_p), st(v_p), st(ki_p), st(c_p), st(n_p), st(m_p), st(conv_p), st(mk_p), st(mv_p),
            st(k_s), st(v_s), st(ki_s), st(c_s), st(n_s), st(m_s), st(conv_s))
```

```python
import functools
import math

import jax
import jax.numpy as jnp
from jax import lax
from jax.experimental import pallas as pl
from jax.experimental.pallas import tpu as pltpu

F32 = jnp.float32
BF16 = jnp.bfloat16

D_MODEL = 2048
CHUNK = 64
M_HEADS, M_DK, M_DV = 8, 128, 128
A_HEADS, A_KV, A_DH = 8, 2, 128
I_HEADS, I_DH = 8, 64
TOPK_MAX = 256
N_MEM, C_HEADS, C_DH = 256, 4, 256
N_BRANCH, BRANCH_W = 3, 1024
D_FF = 5632
CONV_W = 3
ROPE_THETA = 500000.0
ROPE_FRAC = 4
EPS = 1e-6
NEG = -1e30

LANES = 128
SUBLANES = 8
VMEM_LIMIT = 48 * 1024 * 1024

QLANES = LANES
Q_SCALE = A_DH ** -0.5 * math.log2(math.e)

ORIG_SPLITS = (1024, 1024, 1024, 1024, 8, 8, 1024, 256, 256, 512, 64, 8, 1024, 6144)
ORIG_NAMES = ("mq", "mk", "mv", "mo", "mi", "mf", "aq", "ak", "av", "iq", "ik", "iw", "cq", "gt")
MAIN_ORDER = ("mq", "mk", "mv", "mo", "aq", "cq", "gt", "ak", "av", "iq")
SMALL_ORDER = ("ik", "mi", "mf", "iw")
MAIN_W = 13312
COL = {}
_off = 0
for _n in MAIN_ORDER:
    COL[_n] = _off
    _off += ORIG_SPLITS[ORIG_NAMES.index(_n)]
assert _off == MAIN_W
SCOL = {"ik": 0, "mi": 64, "mf": 72, "iw": 80}


def _cparams(sem):
    return pltpu.CompilerParams(dimension_semantics=sem, vmem_limit_bytes=VMEM_LIMIT)


def _dot(a, b):
    return jnp.dot(a, b, preferred_element_type=F32)


def _dot_nt(a, b):
    return lax.dot_general(a, b, (((1,), (1,)), ((), ())), preferred_element_type=F32)


def _dot_tn(a, b):
    return lax.dot_general(a, b, (((0,), (0,)), ((), ())), preferred_element_type=F32)


def _norm_matmul_kernel(x_ref, g_ref, w_ref, o_ref, xn_ref):
    @pl.when(pl.program_id(1) == 0)
    def _():
        x = x_ref[...]
        ms = jnp.mean(x * x, axis=-1, keepdims=True)
        xn_ref[...] = (x * lax.rsqrt(ms + EPS) * g_ref[...]).astype(BF16)

    o_ref[...] = _dot(xn_ref[...], w_ref[...]).astype(o_ref.dtype)


def norm_matmul(x, g, w, tm, tn, name):
    m, k = x.shape
    n = w.shape[1]
    return pl.pallas_call(
        _norm_matmul_kernel,
        out_shape=jax.ShapeDtypeStruct((m, n), F32),
        grid=(m // tm, n // tn),
        in_specs=[pl.BlockSpec((tm, k), lambda i, j: (i, 0)),
                  pl.BlockSpec((1, k), lambda i, j: (0, 0)),
                  pl.BlockSpec((k, tn), lambda i, j: (0, j))],
        out_specs=pl.BlockSpec((tm, tn), lambda i, j: (i, j)),
        scratch_shapes=[pltpu.VMEM((tm, k), BF16)],
        compiler_params=_cparams(("arbitrary", "arbitrary")),
        name=name,
    )(x, g.reshape(1, k), w)


def _matmul_res_kernel(a_ref, w_ref, r_ref, o_ref):
    o_ref[...] = _dot(a_ref[...], w_ref[...]) + r_ref[...]


def matmul_res(a, w, res, tm, tn, name):
    m, k = a.shape
    n = w.shape[1]
    return pl.pallas_call(
        _matmul_res_kernel,
        out_shape=jax.ShapeDtypeStruct((m, n), F32),
        grid=(m // tm, n // tn),
        in_specs=[pl.BlockSpec((tm, k), lambda i, j: (i, 0)),
                  pl.BlockSpec((k, tn), lambda i, j: (0, j)),
                  pl.BlockSpec((tm, tn), lambda i, j: (i, j))],
        out_specs=pl.BlockSpec((tm, tn), lambda i, j: (i, j)),
        compiler_params=_cparams(("arbitrary", "arbitrary")),
        name=name,
    )(a, w, res)


def _rope_tables(pos, dh):
    r = dh // ROPE_FRAC
    half = r // 2
    inv = ROPE_THETA ** (-jnp.arange(half, dtype=F32) * 2.0 / r)
    ang = pos.astype(F32)[:, None] * inv[None, :]
    cos, sin = jnp.cos(ang), jnp.sin(ang)
    t = pos.shape[0]
    ones = jnp.ones((t, dh - r), F32)
    zeros_h = jnp.zeros((t, half), F32)
    zeros_rest = jnp.zeros((t, dh - r), F32)
    c = jnp.concatenate([cos, cos, ones], axis=1)
    sa = jnp.concatenate([zeros_h, sin, zeros_rest], axis=1)
    sb = jnp.concatenate([-sin, zeros_h, zeros_rest], axis=1)
    rep = LANES // dh
    return tuple(jnp.tile(a, (1, rep)) for a in (c, sa, sb))


def _rope_apply(x, c, sa, sb, half):
    return x * c + pltpu.roll(x, half, 1) * sa + pltpu.roll(x, LANES - half, 1) * sb


def _rope_kernel(aq_ref, ak_ref, iq_ref, sm_ref, c128_ref, sa128_ref, sb128_ref, c64_ref, sa64_ref, sb64_ref,
                 q_ref, qi_ref, k_ref, ki_ref):
    c1, a1, b1 = c128_ref[...], sa128_ref[...], sb128_ref[...]
    c2, a2, b2 = c64_ref[...], sa64_ref[...], sb64_ref[...]
    h1 = A_DH // ROPE_FRAC // 2
    h2 = I_DH // ROPE_FRAC // 2
    for h in range(A_HEADS):
        sl = slice(h * LANES, (h + 1) * LANES)
        q_ref[:, sl] = (_rope_apply(aq_ref[:, sl], c1, a1, b1, h1) * Q_SCALE).astype(q_ref.dtype)
    for h in range(A_KV):
        sl = slice(h * LANES, (h + 1) * LANES)
        k_ref[:, sl] = _rope_apply(ak_ref[:, sl], c1, a1, b1, h1)
    for h in range(I_HEADS * I_DH // LANES):
        sl = slice(h * LANES, (h + 1) * LANES)
        qi_ref[:, sl] = _rope_apply(iq_ref[:, sl], c2, a2, b2, h2).astype(qi_ref.dtype)
    ki = _rope_apply(sm_ref[...], c2, a2, b2, h2)
    ki_ref[...] = ki[:, :I_DH]


def rope(p_main, p_small, pos, seq, tr):
    m = p_main.shape[0]
    t128 = _rope_tables(pos, A_DH)
    t64 = _rope_tables(pos, I_DH)
    if seq < tr:
        t128 = tuple(jnp.tile(a, (tr // seq, 1)) for a in t128)
        t64 = tuple(jnp.tile(a, (tr // seq, 1)) for a in t64)
    ntab = max(seq, tr) // tr
    tab_spec = pl.BlockSpec((tr, LANES), lambda i: (i % ntab, 0))
    return pl.pallas_call(
        _rope_kernel,
        out_shape=(jax.ShapeDtypeStruct((m, A_HEADS * A_DH), BF16),
                   jax.ShapeDtypeStruct((m, I_HEADS * I_DH), BF16),
                   jax.ShapeDtypeStruct((m, A_KV * A_DH), F32),
                   jax.ShapeDtypeStruct((m, I_DH), F32)),
        grid=(m // tr,),
        in_specs=[pl.BlockSpec((tr, 1024), lambda i: (i, COL["aq"] // 1024)),
                  pl.BlockSpec((tr, 256), lambda i: (i, COL["ak"] // 256)),
                  pl.BlockSpec((tr, 512), lambda i: (i, COL["iq"] // 512)),
                  pl.BlockSpec((tr, LANES), lambda i: (i, 0)),
                  tab_spec, tab_spec, tab_spec, tab_spec, tab_spec, tab_spec],
        out_specs=(pl.BlockSpec((tr, 1024), lambda i: (i, 0)),
                   pl.BlockSpec((tr, 512), lambda i: (i, 0)),
                   pl.BlockSpec((tr, 256), lambda i: (i, 0)),
                   pl.BlockSpec((tr, I_DH), lambda i: (i, 0))),
        compiler_params=_cparams(("arbitrary",)),
        name="rope",
    )(p_main, p_main, p_main, p_small, *t128, *t64)


def _split3(x):
    hi = x.astype(BF16)
    r1 = x - hi.astype(F32)
    mid = r1.astype(BF16)
    lo = (r1 - mid.astype(F32)).astype(BF16)
    return hi, mid, lo


def _log_sigmoid(x):
    return jnp.minimum(x, 0.0) - jnp.log1p(jnp.exp(-jnp.abs(x)))


def _mlstm_kernel(q_ref, k_ref, v_ref, o_ref, sm_ref, big_r_ref, bfg_r_ref,
                  c0_ref, n0_ref, m0_ref, h_ref, c_ref, n_ref, m_ref):
    L = CHUNK

    @pl.when(pl.program_id(1) == 0)
    def _():
        c_ref[...] = c0_ref[...]
        n_ref[...] = n0_ref[...]
        m_ref[...] = m0_ref[...]

    g = sm_ref[...]
    ig_c = g[:, SCOL["mi"]:SCOL["mi"] + M_HEADS] + big_r_ref[...]
    lf_c = _log_sigmoid(g[:, SCOL["mf"]:SCOL["mf"] + M_HEADS] + bfg_r_ref[...])

    row = lax.broadcasted_iota(jnp.int32, (L, L), 0)
    col = lax.broadcasted_iota(jnp.int32, (L, L), 1)
    causal = col <= row
    tril = jnp.where(causal, 1.0, 0.0).astype(BF16)
    triu = jnp.where(row <= col, 1.0, 0.0).astype(BF16)
    eye = jnp.where(row == col, 1.0, 0.0).astype(BF16)
    lf_parts = _split3(lf_c)
    b_c = sum(_dot(tril, part) for part in lf_parts)
    b_r = sum(_dot_tn(part, triu) for part in lf_parts)
    ig_r = sum(_dot_tn(part, eye) for part in _split3(ig_c))

    scale = M_DK ** -0.5
    for h in range(M_HEADS):
        sl = slice(h * LANES, (h + 1) * LANES)
        bc = b_c[:, h:h + 1]
        br = b_r[h:h + 1, :]
        igr = ig_r[h:h + 1, :]
        igc = ig_c[:, h:h + 1]
        m_prev = m_ref[0, h:h + 1, 0:1]
        c_prev = c_ref[0, h]
        n_prev = n_ref[0, h:h + 1, :]

        log_d = jnp.where(causal, bc - br + igr, -jnp.inf)
        m_inter = bc + m_prev
        m_t = jnp.maximum(m_inter, jnp.max(log_d, axis=-1, keepdims=True))
        d = jnp.exp(log_d - m_t)
        inter = jnp.exp(m_inter - m_t)

        qh = q_ref[:, sl]
        kh = k_ref[:, sl] * scale
        vh = v_ref[:, sl]
        qb = qh.astype(BF16)
        kb = kh.astype(BF16)
        s = _dot_nt(qb, kb) * d
        num = _dot(s.astype(BF16), vh.astype(BF16)) + _dot_nt(qb, c_prev.astype(BF16)) * inter
        den = jnp.sum(s, axis=-1, keepdims=True) + inter * jnp.sum(qh * n_prev, axis=-1, keepdims=True)
        hout = num / jnp.maximum(jnp.abs(den), jnp.exp(-m_t))
        gate = 1.0 / (1.0 + jnp.exp(-o_ref[:, sl]))
        h_ref[:, sl] = (hout * gate).astype(h_ref.dtype)

        m_new = m_t[L - 1:L, :]
        b_last = bc[L - 1:L, :]
        w_c = jnp.exp(b_last - bc + igc - m_new)
        decay = jnp.exp(b_last + m_prev - m_new)
        c_ref[0, h] = decay * c_prev + _dot_tn((vh * w_c).astype(BF16), kb)
        n_ref[0, h:h + 1, :] = decay * n_prev + jnp.sum(kh * w_c, axis=0, keepdims=True)
        m_ref[0, h:h + 1, :] = jnp.broadcast_to(m_new, (1, LANES))


def mlstm(p_main, p_small, b_ig, b_fg, c0, n0, m0, batch, seq):
    nc = seq // CHUNK
    m = batch * seq
    rowblk = lambda cb: pl.BlockSpec((CHUNK, 1024), lambda b, c: (b * nc + c, cb))
    st4 = pl.BlockSpec((1, M_HEADS, M_DV, M_DK), lambda b, c: (b, 0, 0, 0))
    st3 = pl.BlockSpec((1, M_HEADS, LANES), lambda b, c: (b, 0, 0))
    vec_r = pl.BlockSpec((1, M_HEADS), lambda b, c: (0, 0))
    return pl.pallas_call(
        _mlstm_kernel,
        out_shape=(jax.ShapeDtypeStruct((m, M_HEADS * M_DV), BF16),
                   jax.ShapeDtypeStruct((batch, M_HEADS, M_DV, M_DK), F32),
                   jax.ShapeDtypeStruct((batch, M_HEADS, M_DK), F32),
                   jax.ShapeDtypeStruct((batch, M_HEADS, LANES), F32)),
        grid=(batch, nc),
        in_specs=[rowblk(COL["mq"] // 1024), rowblk(COL["mk"] // 1024), rowblk(COL["mv"] // 1024),
                  rowblk(COL["mo"] // 1024),
                  pl.BlockSpec((CHUNK, LANES), lambda b, c: (b * nc + c, 0)),
                  vec_r, vec_r, st4, st3, st3],
        out_specs=(pl.BlockSpec((CHUNK, 1024), lambda b, c: (b * nc + c, 0)), st4, st3, st3),
        compiler_params=_cparams(("arbitrary", "arbitrary")),
        name="mlstm",
    )(p_main, p_main, p_main, p_main, p_small,
      b_ig.reshape(1, M_HEADS), b_fg.reshape(1, M_HEADS), c0, n0, m0)


def _col_reduce(x, op, final):
    r = x.reshape(x.shape[0] // SUBLANES, SUBLANES, x.shape[1])
    while r.shape[0] > 1:
        half = r.shape[0] // 2
        s = op(r[:half], r[half:2 * half])
        r = s if r.shape[0] % 2 == 0 else jnp.concatenate([s, r[2 * half:]], axis=0)
    return final(r[0], axis=0, keepdims=True)


def _colsum(x):
    return _col_reduce(x, jnp.add, jnp.sum)


def _colmax(x):
    return _col_reduce(x, jnp.maximum, jnp.max)


def _count0(mask):
    return _colsum(jnp.where(mask, 1.0, 0.0))


def _dsa_body(q_ref, qi_ref, sm_ref, out_ref, kcat_s, vt_s, ki_s, sel_s, *, sc, s_real, nb, tq, base_pos, topk, qb):
    hpg = A_HEADS // A_KV
    lane = lax.broadcasted_iota(jnp.int32, (1, QLANES), 1)
    qpos = base_pos + qb * tq + lane % tq
    chunk_end = (qpos // CHUNK + 1) * CHUNK
    sidx = lax.broadcasted_iota(jnp.int32, (sc, 1), 0)
    admissible = sidx < chunk_end
    row_b = lax.broadcasted_iota(jnp.int32, (QLANES, 1), 0) // tq

    def slot_rows(x, bb):
        return x if nb == 1 else jnp.where(row_b == bb, x, 0.0)

    wit = sm_ref[...].T[SCOL["iw"]:SCOL["iw"] + I_HEADS, :]
    score = None
    for h in range(I_HEADS):
        qh = qi_ref[:, h * I_DH:(h + 1) * I_DH].astype(F32) * (I_DH ** -0.5)
        lg = None
        for bb in range(nb):
            part = _dot_nt(ki_s[bb, 0:sc, :], slot_rows(qh, bb).astype(BF16))
            lg = part if lg is None else lg + part
        term = jnp.maximum(lg, 0.0) * wit[h:h + 1, :]
        score = term if score is None else score + term
    score = score * (I_HEADS ** -0.5) + 0.0
    score = jnp.where(admissible, score, NEG)
    if sc > s_real:
        score = jnp.where(sidx < s_real, score, -jnp.inf)

    bits = pltpu.bitcast(score, jnp.int32)
    key = jnp.where(bits < 0, bits ^ jnp.int32(0x7FFFFFFF), bits)
    kf = float(topk)

    def thr_body(i, ans):
        cand = ans + lax.shift_left(jnp.int32(1), jnp.int32(31) - i)
        ok = _count0(key >= cand) >= kf
        return jnp.where(ok, cand, ans)

    thr = lax.fori_loop(0, 32, thr_body, jnp.full((1, QLANES), jnp.iinfo(jnp.int32).min, jnp.int32))
    gt_mask = key > thr
    eq_mask = key == thr
    need = kf - _count0(gt_mask)
    n_eq = _count0(eq_mask)
    sel_s[0:sc, :] = jnp.where(admissible, jnp.where(key >= thr, 1.0, 0.0), 0.0)

    @pl.when(jnp.max(n_eq - need) > 0.0)
    def _():
        def idx_body(i, j):
            cand = j + lax.shift_left(jnp.int32(1), jnp.int32(12) - i)
            ok = _count0(eq_mask & (sidx < cand)) < need
            return jnp.where(ok, cand, j)

        jstar = lax.fori_loop(0, 13, idx_body, jnp.zeros((1, QLANES), jnp.int32))
        sel = gt_mask | (eq_mask & (sidx <= jstar))
        sel_s[0:sc, :] = jnp.where(admissible, jnp.where(sel, 1.0, 0.0), 0.0)

    mask = sel_s[0:sc, :] > 0.5

    lane4_b = (lax.broadcasted_iota(jnp.int32, (1, hpg * QLANES), 1) % QLANES) // tq
    row4_b = (lax.broadcasted_iota(jnp.int32, (hpg * QLANES, 1), 0) % QLANES) // tq
    for j in range(A_KV):
        q4 = jnp.concatenate([q_ref[:, (j * hpg + g) * A_DH:(j * hpg + g + 1) * A_DH] for g in range(hpg)], axis=0)
        if nb > 1:
            q4f = q4.astype(F32)
            q4 = jnp.concatenate([jnp.where(row4_b == bb, q4f, 0.0).astype(BF16) for bb in range(nb)], axis=1)
        st = _dot_nt(kcat_s[j, 0:sc, :], q4)
        ps, ls = [], []
        for g in range(hpg):
            s = jnp.where(mask, st[:, g * QLANES:(g + 1) * QLANES], NEG)
            mx = _colmax(s)
            p = jnp.exp2(s - mx)
            ls.append(_colsum(p))
            ps.append(p.astype(BF16))
        p4 = jnp.concatenate(ps, axis=1)
        l4 = jnp.concatenate(ls, axis=1)
        o4 = None
        for bb in range(nb):
            ob = _dot(vt_s[j, bb, :, 0:sc], p4)
            o4 = ob if o4 is None else jnp.where(lane4_b == bb, ob, o4)
        o4 = o4 / l4
        for g in range(hpg):
            h = j * hpg + g
            out_ref[:, h * A_DH:(h + 1) * A_DH] = o4[:, g * QLANES:(g + 1) * QLANES].T.astype(out_ref.dtype)


def _dsa_kernel(*refs, tq, nb, s_real, s_pad, s_past, seq, base_pos, topk, classes, steps_per_class):
    if s_past:
        (q_ref, qi_ref, sm_ref, kn_ref, vn_ref, kin_ref, pk_ref, pv_ref, pki_ref,
         out_ref, kcat_s, vt_s, ki_s, sel_s, vtmp_s) = refs
    else:
        (q_ref, qi_ref, sm_ref, kn_ref, vn_ref, kin_ref, out_ref, kcat_s, vt_s, ki_s, sel_s, vtmp_s) = refs
    qb = pl.program_id(1)

    @pl.when(qb == 0)
    def _():
        for bb in range(nb):
            rows = slice(bb * seq, (bb + 1) * seq)
            if s_past:
                ki_s[bb, 0:s_past, :] = pki_ref[bb].astype(BF16)
            ki_s[bb, s_past:s_real, :] = kin_ref[rows, :].astype(BF16)
            if s_pad > s_real:
                ki_s[bb, s_real:s_pad, :] = jnp.zeros((s_pad - s_real, I_DH), BF16)
            for j in range(A_KV):
                cols = slice(j * A_DH, (j + 1) * A_DH)
                lanes = slice(bb * A_DH, (bb + 1) * A_DH)
                if s_past:
                    kcat_s[j, 0:s_past, lanes] = pk_ref[bb, :, cols].astype(BF16)
                    vtmp_s[0:s_past, :] = pv_ref[bb, :, cols]
                kcat_s[j, s_past:s_real, lanes] = kn_ref[rows, cols].astype(BF16)
                vtmp_s[s_past:s_real, :] = vn_ref[rows, cols]
                if s_pad > s_real:
                    kcat_s[j, s_real:s_pad, lanes] = jnp.zeros((s_pad - s_real, A_DH), BF16)
                    vtmp_s[s_real:s_pad, :] = jnp.zeros((s_pad - s_real, A_DH), F32)
                vt_s[j, bb] = vtmp_s[...].T.astype(BF16)

    body = functools.partial(_dsa_body, q_ref, qi_ref, sm_ref, out_ref, kcat_s, vt_s, ki_s, sel_s,
                             s_real=s_real, nb=nb, tq=tq, base_pos=base_pos, topk=topk, qb=qb)
    if len(classes) == 1:
        body(sc=classes[0])
    else:
        for c, sc in enumerate(classes):
            pl.when(qb // steps_per_class == c)(functools.partial(body, sc=sc))


def dsa(q_r, qi_r, p_small, k_new, v_src, v_colblk, ki_new, past, batch, seq, base_pos):
    m = batch * seq
    tq = min(seq, QLANES)
    nb = QLANES // tq
    nq = seq // tq
    s_past = 0 if past is None else past[0].shape[1]
    s_real = s_past + seq
    s_pad = -(-s_real // LANES) * LANES
    topk = min(TOPK_MAX, s_real // 4)
    assert topk <= s_real and s_pad < 8192 and batch % nb == 0
    n_cls = 4 if (past is None and nb == 1 and nq % 4 == 0 and s_pad % (4 * LANES) == 0) else 1
    classes = tuple(s_pad * (c + 1) // n_cls for c in range(n_cls))
    assert n_cls == 1 or base_pos == 0
    qrow = lambda w: pl.BlockSpec((QLANES, w), lambda b, i: (b * nq + i, 0))
    in_specs = [qrow(1024), qrow(512), qrow(LANES),
                pl.BlockSpec((nb * seq, 256), lambda b, i: (b, 0)),
                pl.BlockSpec((nb * seq, 256), lambda b, i: (b, v_colblk)),
                pl.BlockSpec((nb * seq, I_DH), lambda b, i: (b, 0))]
    args = [q_r, qi_r, p_small, k_new, v_src, ki_new]
    if past is not None:
        in_specs += [pl.BlockSpec((nb, s_past, 256), lambda b, i: (b, 0, 0)),
                     pl.BlockSpec((nb, s_past, 256), lambda b, i: (b, 0, 0)),
                     pl.BlockSpec((nb, s_past, I_DH), lambda b, i: (b, 0, 0))]
        args += list(past)
    kern = functools.partial(_dsa_kernel, tq=tq, nb=nb, s_real=s_real, s_pad=s_pad, s_past=s_past, seq=seq,
                             base_pos=base_pos, topk=topk, classes=classes, steps_per_class=nq // n_cls)
    return pl.pallas_call(
        kern,
        out_shape=jax.ShapeDtypeStruct((m, A_HEADS * A_DH), BF16),
        grid=(batch // nb, nq),
        in_specs=in_specs,
        out_specs=pl.BlockSpec((QLANES, 1024), lambda b, i: (b * nq + i, 0)),
        scratch_shapes=[pltpu.VMEM((A_KV, s_pad, nb * A_DH), BF16),
                        pltpu.VMEM((A_KV, nb, A_DH, s_pad), BF16),
                        pltpu.VMEM((nb, s_pad, I_DH), BF16),
                        pltpu.VMEM((s_pad, QLANES), F32),
                        pltpu.VMEM((s_pad, A_DH), F32)],
        compiler_params=_cparams(("arbitrary", "arbitrary")),
        name="dsa",
    )(*args)


def _mem_attn_kernel(q_ref, k_ref, v_ref, o_ref):
    for h in range(C_HEADS):
        sl = slice(h * C_DH, (h + 1) * C_DH)
        s = _dot_nt(q_ref[:, sl].astype(BF16), k_ref[:, sl].astype(BF16)) * (C_DH ** -0.5)
        mx = jnp.max(s, axis=-1, keepdims=True)
        p = jnp.exp(s - mx)
        l = jnp.sum(p, axis=-1, keepdims=True)
        o_ref[:, sl] = (_dot(p.astype(BF16), v_ref[:, sl].astype(BF16)) / l).astype(o_ref.dtype)


def mem_attn(p_main, mk, mk_colblk, mv, mv_colblk, batch, seq, tq):
    m = batch * seq
    nq = seq // tq
    w = C_HEADS * C_DH
    return pl.pallas_call(
        _mem_attn_kernel,
        out_shape=jax.ShapeDtypeStruct((m, w), BF16),
        grid=(batch, nq),
        in_specs=[pl.BlockSpec((tq, w), lambda b, i: (b * nq + i, COL["cq"] // w)),
                  pl.BlockSpec((N_MEM, w), lambda b, i: (b, mk_colblk)),
                  pl.BlockSpec((N_MEM, w), lambda b, i: (b, mv_colblk))],
        out_specs=pl.BlockSpec((tq, w), lambda b, i: (b * nq + i, 0)),
        compiler_params=_cparams(("arbitrary", "arbitrary")),
        name="mem_attn",
    )(p_main, mk, mv)


def _merge_kernel(a_ref, b_ref, c_ref, ga_ref, gb_ref, gc_ref, w_ref, o_ref):
    acc = None
    for br_ref, g_ref, n in ((a_ref, ga_ref, 0), (b_ref, gb_ref, 1), (c_ref, gc_ref, 2)):
        proj = _dot(br_ref[...], w_ref[n])
        term = proj / (1.0 + jnp.exp(-g_ref[...]))
        acc = term if acc is None else acc + term
    o_ref[...] = acc.astype(o_ref.dtype)


def merge(br_a, br_b, br_c, p_main, w_branch, tm, tn):
    m = br_a.shape[0]
    nj = D_MODEL // tn
    brs = pl.BlockSpec((tm, BRANCH_W), lambda i, j: (i, 0))
    gate = lambda n: pl.BlockSpec((tm, tn), lambda i, j: (i, (COL["gt"] + n * D_MODEL) // tn + j))
    return pl.pallas_call(
        _merge_kernel,
        out_shape=jax.ShapeDtypeStruct((m, D_MODEL), BF16),
        grid=(m // tm, nj),
        in_specs=[brs, brs, brs, gate(0), gate(1), gate(2),
                  pl.BlockSpec((N_BRANCH, BRANCH_W, tn), lambda i, j: (0, 0, j))],
        out_specs=pl.BlockSpec((tm, tn), lambda i, j: (i, j)),
        compiler_params=_cparams(("arbitrary", "arbitrary")),
        name="merge",
    )(br_a, br_b, br_c, p_main, p_main, p_main, w_branch)


def _ffn_up_kernel(h_ref, g_ref, wg_ref, wv_ref, cwg_ref, cwv_ref, cbg_ref, cbv_ref, pg_ref, pv_ref,
                   act_ref, tg_ref, tv_ref, xn_ref, ug_s, uv_s, cg_s, cv_s, *, tm, seg, blocks_per_seq):
    i = pl.program_id(0)
    j = pl.program_id(1)
    nseg = tm // seg
    hdr = SUBLANES

    @pl.when(j == 0)
    def _():
        x = h_ref[...]
        ms = jnp.mean(x * x, axis=-1, keepdims=True)
        xn_ref[...] = (x * lax.rsqrt(ms + EPS) * g_ref[...]).astype(BF16)

    xn = xn_ref[...]
    ug = _dot(xn, wg_ref[...])
    uv = _dot(xn, wv_ref[...])

    if blocks_per_seq > 1:
        @pl.when((i % blocks_per_seq) == 0)
        def _():
            cg_s[j] = pg_ref[0]
            cv_s[j] = pv_ref[0]

    def conv(useg, prev, cw_ref, cb_ref, u_s):
        u_s[hdr - 2:hdr, :] = prev
        u_s[hdr:hdr + seg, :] = useg
        return (cb_ref[...] + u_s[hdr - 2:hdr - 2 + seg, :] * cw_ref[0:1, :]
                + u_s[hdr - 1:hdr - 1 + seg, :] * cw_ref[1:2, :] + useg * cw_ref[2:3, :])

    for s in range(nseg):
        rows = slice(s * seg, (s + 1) * seg)
        ugs, uvs = ug[rows, :], uv[rows, :]
        prev_g = cg_s[j] if blocks_per_seq > 1 else pg_ref[s]
        prev_v = cv_s[j] if blocks_per_seq > 1 else pv_ref[s]
        cg = conv(ugs, prev_g, cwg_ref, cbg_ref, ug_s)
        cv = conv(uvs, prev_v, cwv_ref, cbv_ref, uv_s)
        act_ref[rows, :] = (cg / (1.0 + jnp.exp(-cg)) * cv).astype(act_ref.dtype)
        tail_g, tail_v = ugs[seg - 2:seg, :], uvs[seg - 2:seg, :]
        tg_ref[s] = tail_g
        tv_ref[s] = tail_v
        if blocks_per_seq > 1:
            cg_s[j] = tail_g
            cv_s[j] = tail_v


def ffn_up(h, g_ffn, w_up, conv_w, conv_b, conv_prev, seq, tm, tn):
    m = h.shape[0]
    seg = min(tm, seq)
    nseg = tm // seg
    bps = seq // seg
    blocks_per_seq = bps if nseg == 1 else 1
    nj = D_FF // tn
    prev_spec = lambda off: pl.BlockSpec((nseg, CONV_W - 1, tn),
                                         lambda i, j: ((i // blocks_per_seq) if nseg == 1 else i, 0, off + j))
    tail_spec = pl.BlockSpec((nseg, CONV_W - 1, tn), lambda i, j: (i, 0, j))
    kern = functools.partial(_ffn_up_kernel, tm=tm, seg=seg, blocks_per_seq=blocks_per_seq)
    return pl.pallas_call(
        kern,
        out_shape=(jax.ShapeDtypeStruct((m, D_FF), BF16),
                   jax.ShapeDtypeStruct((m // seg, CONV_W - 1, D_FF), F32),
                   jax.ShapeDtypeStruct((m // seg, CONV_W - 1, D_FF), F32)),
        grid=(m // tm, nj),
        in_specs=[pl.BlockSpec((tm, D_MODEL), lambda i, j: (i, 0)),
                  pl.BlockSpec((1, D_MODEL), lambda i, j: (0, 0)),
                  pl.BlockSpec((D_MODEL, tn), lambda i, j: (0, j)),
                  pl.BlockSpec((D_MODEL, tn), lambda i, j: (0, nj + j)),
                  pl.BlockSpec((CONV_W, tn), lambda i, j: (0, j)),
                  pl.BlockSpec((CONV_W, tn), lambda i, j: (0, nj + j)),
                  pl.BlockSpec((1, tn), lambda i, j: (0, j)),
                  pl.BlockSpec((1, tn), lambda i, j: (0, nj + j)),
                  prev_spec(0), prev_spec(nj)],
        out_specs=(pl.BlockSpec((tm, tn), lambda i, j: (i, j)), tail_spec, tail_spec),
        scratch_shapes=[pltpu.VMEM((tm, D_MODEL), BF16),
                        pltpu.VMEM((SUBLANES + seg, tn), F32), pltpu.VMEM((SUBLANES + seg, tn), F32),
                        pltpu.VMEM((nj, CONV_W - 1, tn), F32), pltpu.VMEM((nj, CONV_W - 1, tn), F32)],
        compiler_params=_cparams(("arbitrary", "arbitrary")),
        name="ffn_up",
    )(h, g_ffn.reshape(1, D_MODEL), w_up, w_up, conv_w, conv_w,
      conv_b.reshape(1, 2 * D_FF), conv_b.reshape(1, 2 * D_FF), conv_prev, conv_prev)


def _ffn_down_kernel(a_ref, w_ref, h_ref, g_ref, y_ref, acc_ref):
    k = pl.program_id(1)
    part = _dot(a_ref[...], w_ref[...])

    @pl.when(k == 0)
    def _():
        acc_ref[...] = part

    @pl.when(k > 0)
    def _():
        acc_ref[...] += part

    @pl.when(k == pl.num_programs(1) - 1)
    def _():
        z = h_ref[...] + acc_ref[...]
        ms = jnp.mean(z * z, axis=-1, keepdims=True)
        y_ref[...] = z * lax.rsqrt(ms + EPS) * g_ref[...]


def ffn_down(act, w_down, h, g_final, tm, tk):
    m = act.shape[0]
    return pl.pallas_call(
        _ffn_down_kernel,
        out_shape=jax.ShapeDtypeStruct((m, D_MODEL), F32),
        grid=(m // tm, D_FF // tk),
        in_specs=[pl.BlockSpec((tm, tk), lambda i, k: (i, k)),
                  pl.BlockSpec((tk, D_MODEL), lambda i, k: (k, 0)),
                  pl.BlockSpec((tm, D_MODEL), lambda i, k: (i, 0)),
                  pl.BlockSpec((1, D_MODEL), lambda i, k: (0, 0))],
        out_specs=pl.BlockSpec((tm, D_MODEL), lambda i, k: (i, 0)),
        scratch_shapes=[pltpu.VMEM((tm, D_MODEL), F32)],
        compiler_params=_cparams(("arbitrary", "arbitrary")),
        name="ffn_down",
    )(act, w_down, h, g_final.reshape(1, D_MODEL))


def _group(x, pos, past, mstate, mem_k, mem_kblk, mem_v, mem_vblk, conv_prev, wts, base_pos):
    (g_mix, w_main, w_small, b_ig, b_fg, w_branch, w_out, g_ffn, w_up, conv_w, conv_b, w_down, g_final) = wts
    batch, seq, _ = x.shape
    m = batch * seq
    x2 = x.reshape(m, D_MODEL)

    p_main = norm_matmul(x2, g_mix, w_main, 1024, 1024, "in_proj")
    p_small = norm_matmul(x2, g_mix, w_small, 1024, LANES, "in_proj_small")

    q_r, qi_r, k_new, ki_new = rope(p_main, p_small, pos, seq, 512)
    v_new = p_main[:, COL["av"]:COL["av"] + A_KV * A_DH]

    c0, n0, m0 = mstate
    br_a, c_new, n_new, m_new = mlstm(p_main, p_small, b_ig, b_fg, c0, n0,
                                      jnp.broadcast_to(m0[:, :, None], (batch, M_HEADS, LANES)), batch, seq)
    br_b = dsa(q_r, qi_r, p_small, k_new, p_main, COL["av"] // 256, ki_new, past, batch, seq, base_pos)
    br_c = mem_attn(p_main, mem_k, mem_kblk, mem_v, mem_vblk, batch, seq, min(seq, 512))

    merged = merge(br_a, br_b, br_c, p_main, w_branch, 1024, 512)
    h = matmul_res(merged, w_out, x2, 1024, 1024, "out_proj")

    act, tail_g, tail_v = ffn_up(h, g_ffn, w_up, conv_w, conv_b, conv_prev, seq, 1024, 512)
    y = ffn_down(act, w_down, h, g_final, 512, 512)

    nblk = tail_g.shape[0] // batch
    conv_new = jnp.concatenate([tail_g.reshape(batch, nblk, CONV_W - 1, D_FF)[:, -1],
                                tail_v.reshape(batch, nblk, CONV_W - 1, D_FF)[:, -1]], axis=-1)
    outs = (y.reshape(batch, seq, D_MODEL),
            k_new.reshape(batch, seq, A_KV, A_DH), v_new.reshape(batch, seq, A_KV, A_DH),
            ki_new.reshape(batch, seq, I_DH), c_new, n_new, m_new[:, :, 0], conv_new)
    return outs


def kernel(x_prompt, x_sample, mem_prompt, cache_dsa_k, cache_dsa_v, cache_dsa_kidx, state_mlstm_C, state_mlstm_n, state_mlstm_m, state_ffn_conv, cache_mem_k, cache_mem_v, g_mix, w_in, b_igate, b_fgate, w_branch, w_out, g_mem, w_mem_kv, g_ffn, w_up, conv_w, conv_b, w_down, g_final):
    depth = w_in.shape[0]
    assert depth == 1
    B, T = x_prompt.shape[:2]
    DB, DT = x_sample.shape[:2]
    P = cache_dsa_k.shape[2]
    l = 0

    w = w_in[l]
    segs = {}
    off = 0
    for name, width in zip(ORIG_NAMES, ORIG_SPLITS):
        segs[name] = w[:, off:off + width]
        off += width
    w_main = jnp.concatenate([segs[n] for n in MAIN_ORDER], axis=1).astype(BF16)
    small_w = sum(ORIG_SPLITS[ORIG_NAMES.index(n)] for n in SMALL_ORDER)
    w_small = jnp.concatenate([segs[n] for n in SMALL_ORDER] + [jnp.zeros((D_MODEL, LANES - small_w), F32)],
                              axis=1).astype(BF16)
    wts = (g_mix[l], w_main, w_small, b_igate[l], b_fgate[l], w_branch[l].astype(BF16), w_out[l].astype(BF16),
           g_ffn[l], w_up[l].astype(BF16), conv_w[l], conv_b[l], w_down[l].astype(BF16), g_final)

    mkv = norm_matmul(mem_prompt.reshape(B * N_MEM, D_MODEL), g_mem[l], w_mem_kv[l].astype(BF16), 512, 1024, "mem_kv")
    zero_state = (jnp.zeros((B, M_HEADS, M_DV, M_DK), F32), jnp.zeros((B, M_HEADS, M_DK), F32),
                  jnp.zeros((B, M_HEADS), F32))
    zero_conv = jnp.zeros((B, CONV_W - 1, 2 * D_FF), F32)
    (y_p, k_p, v_p, ki_p, c_p, n_p, m_p, conv_p) = _group(
        x_prompt, jnp.arange(T, dtype=jnp.int32), None, zero_state, mkv, 0, mkv, 1, zero_conv, wts, 0)
    hw = C_HEADS * C_DH
    mk_p = mkv[:, :hw].reshape(B, N_MEM, C_HEADS, C_DH)
    mv_p = mkv[:, hw:].reshape(B, N_MEM, C_HEADS, C_DH)

    past = (cache_dsa_k[l].reshape(DB, P, A_KV * A_DH), cache_dsa_v[l].reshape(DB, P, A_KV * A_DH),
            cache_dsa_kidx[l])
    mstate = (state_mlstm_C[l], state_mlstm_n[l], state_mlstm_m[l])
    (y_s, k_s, v_s, ki_s, c_s, n_s, m_s, conv_s) = _group(
        x_sample, P + jnp.arange(DT, dtype=jnp.int32), past, mstate,
        cache_mem_k[l].reshape(DB * N_MEM, hw), 0, cache_mem_v[l].reshape(DB * N_MEM, hw), 0,
        state_ffn_conv[l], wts, P)

    st = lambda a: a[None]
    return (y_p, y_s,
            st(k_p), st(v_p), st(ki_p), st(c_p), st(n_p), st(m_p), st(conv_p), st(mk_p), st(mv_p),
            st(k_s), st(v_s), st(ki_s), st(c_s), st(n_s), st(m_s), st(conv_s))
```

```python
import functools
import math

import jax
import jax.numpy as jnp
from jax import lax
from jax.experimental import pallas as pl
from jax.experimental.pallas import tpu as pltpu

F32 = jnp.float32
BF16 = jnp.bfloat16

D_MODEL = 2048
CHUNK = 64
MLSTM_CHUNK = 256
M_HEADS, M_DK, M_DV = 8, 128, 128
A_HEADS, A_KV, A_DH = 8, 2, 128
I_HEADS, I_DH = 8, 64
TOPK_MAX = 256
N_MEM, C_HEADS, C_DH = 256, 4, 256
N_BRANCH, BRANCH_W = 3, 1024
D_FF = 5632
CONV_W = 3
ROPE_THETA = 500000.0
ROPE_FRAC = 4
EPS = 1e-6
NEG = -1e30

LANES = 128
SUBLANES = 8
VMEM_LIMIT = 48 * 1024 * 1024
FFN_VMEM_LIMIT = 56 * 1024 * 1024

QLANES = LANES
Q_SCALE = A_DH ** -0.5 * math.log2(math.e)

ORIG_SPLITS = (1024, 1024, 1024, 1024, 8, 8, 1024, 256, 256, 512, 64, 8, 1024, 6144)
ORIG_NAMES = ("mq", "mk", "mv", "mo", "mi", "mf", "aq", "ak", "av", "iq", "ik", "iw", "cq", "gt")
MAIN_ORDER = ("mq", "mk", "mv", "mo", "aq", "cq", "gt", "ak", "av", "iq")
SMALL_ORDER = ("ik", "mi", "mf", "iw")
MAIN_W = 13312
COL = {}
_off = 0
for _n in MAIN_ORDER:
    COL[_n] = _off
    _off += ORIG_SPLITS[ORIG_NAMES.index(_n)]
assert _off == MAIN_W
SCOL = {"ik": 0, "mi": 64, "mf": 72, "iw": 80}


def _cparams(sem):
    return pltpu.CompilerParams(dimension_semantics=sem, vmem_limit_bytes=VMEM_LIMIT)


def _dot(a, b):
    return jnp.dot(a, b, preferred_element_type=F32)


def _dot_nt(a, b):
    return lax.dot_general(a, b, (((1,), (1,)), ((), ())), preferred_element_type=F32)


def _dot_tn(a, b):
    return lax.dot_general(a, b, (((0,), (0,)), ((), ())), preferred_element_type=F32)


def _norm_matmul_kernel(x_ref, g_ref, w_ref, o_ref, xn_ref):
    @pl.when(pl.program_id(1) == 0)
    def _():
        x = x_ref[...]
        ms = jnp.mean(x * x, axis=-1, keepdims=True)
        xn_ref[...] = (x * lax.rsqrt(ms + EPS) * g_ref[...]).astype(BF16)

    o_ref[...] = _dot(xn_ref[...], w_ref[...]).astype(o_ref.dtype)


def norm_matmul(x, g, w, tm, tn, name):
    m, k = x.shape
    n = w.shape[1]
    return pl.pallas_call(
        _norm_matmul_kernel,
        out_shape=jax.ShapeDtypeStruct((m, n), F32),
        grid=(m // tm, n // tn),
        in_specs=[pl.BlockSpec((tm, k), lambda i, j: (i, 0)),
                  pl.BlockSpec((1, k), lambda i, j: (0, 0)),
                  pl.BlockSpec((k, tn), lambda i, j: (0, j))],
        out_specs=pl.BlockSpec((tm, tn), lambda i, j: (i, j)),
        scratch_shapes=[pltpu.VMEM((tm, k), BF16)],
        compiler_params=_cparams(("arbitrary", "arbitrary")),
        name=name,
    )(x, g.reshape(1, k), w)


def _matmul_res_kernel(a_ref, w_ref, r_ref, o_ref):
    o_ref[...] = _dot(a_ref[...], w_ref[...]) + r_ref[...]


def matmul_res(a, w, res, tm, tn, name):
    m, k = a.shape
    n = w.shape[1]
    return pl.pallas_call(
        _matmul_res_kernel,
        out_shape=jax.ShapeDtypeStruct((m, n), F32),
        grid=(m // tm, n // tn),
        in_specs=[pl.BlockSpec((tm, k), lambda i, j: (i, 0)),
                  pl.BlockSpec((k, tn), lambda i, j: (0, j)),
                  pl.BlockSpec((tm, tn), lambda i, j: (i, j))],
        out_specs=pl.BlockSpec((tm, tn), lambda i, j: (i, j)),
        compiler_params=_cparams(("arbitrary", "arbitrary")),
        name=name,
    )(a, w, res)


def _rope_tables(pos, dh):
    r = dh // ROPE_FRAC
    half = r // 2
    inv = ROPE_THETA ** (-jnp.arange(half, dtype=F32) * 2.0 / r)
    ang = pos.astype(F32)[:, None] * inv[None, :]
    cos, sin = jnp.cos(ang), jnp.sin(ang)
    t = pos.shape[0]
    ones = jnp.ones((t, dh - r), F32)
    zeros_h = jnp.zeros((t, half), F32)
    zeros_rest = jnp.zeros((t, dh - r), F32)
    c = jnp.concatenate([cos, cos, ones], axis=1)
    sa = jnp.concatenate([zeros_h, sin, zeros_rest], axis=1)
    sb = jnp.concatenate([-sin, zeros_h, zeros_rest], axis=1)
    rep = LANES // dh
    return tuple(jnp.tile(a, (1, rep)) for a in (c, sa, sb))


def _rope_apply(x, c, sa, sb, half):
    return x * c + pltpu.roll(x, half, 1) * sa + pltpu.roll(x, LANES - half, 1) * sb


def _rope_kernel(aq_ref, ak_ref, iq_ref, sm_ref, c128_ref, sa128_ref, sb128_ref, c64_ref, sa64_ref, sb64_ref,
                 q_ref, qi_ref, k_ref, ki_ref):
    c1, a1, b1 = c128_ref[...], sa128_ref[...], sb128_ref[...]
    c2, a2, b2 = c64_ref[...], sa64_ref[...], sb64_ref[...]
    h1 = A_DH // ROPE_FRAC // 2
    h2 = I_DH // ROPE_FRAC // 2
    for h in range(A_HEADS):
        sl = slice(h * LANES, (h + 1) * LANES)
        q_ref[:, sl] = (_rope_apply(aq_ref[:, sl], c1, a1, b1, h1) * Q_SCALE).astype(q_ref.dtype)
    for h in range(A_KV):
        sl = slice(h * LANES, (h + 1) * LANES)
        k_ref[:, sl] = _rope_apply(ak_ref[:, sl], c1, a1, b1, h1)
    for h in range(I_HEADS * I_DH // LANES):
        sl = slice(h * LANES, (h + 1) * LANES)
        qi_ref[:, sl] = _rope_apply(iq_ref[:, sl], c2, a2, b2, h2).astype(qi_ref.dtype)
    ki = _rope_apply(sm_ref[...], c2, a2, b2, h2)
    ki_ref[...] = ki[:, :I_DH]


def rope(p_main, p_small, pos, seq, tr):
    m = p_main.shape[0]
    t128 = _rope_tables(pos, A_DH)
    t64 = _rope_tables(pos, I_DH)
    if seq < tr:
        t128 = tuple(jnp.tile(a, (tr // seq, 1)) for a in t128)
        t64 = tuple(jnp.tile(a, (tr // seq, 1)) for a in t64)
    ntab = max(seq, tr) // tr
    tab_spec = pl.BlockSpec((tr, LANES), lambda i: (i % ntab, 0))
    return pl.pallas_call(
        _rope_kernel,
        out_shape=(jax.ShapeDtypeStruct((m, A_HEADS * A_DH), BF16),
                   jax.ShapeDtypeStruct((m, I_HEADS * I_DH), BF16),
                   jax.ShapeDtypeStruct((m, A_KV * A_DH), F32),
                   jax.ShapeDtypeStruct((m, I_DH), F32)),
        grid=(m // tr,),
        in_specs=[pl.BlockSpec((tr, 1024), lambda i: (i, COL["aq"] // 1024)),
                  pl.BlockSpec((tr, 256), lambda i: (i, COL["ak"] // 256)),
                  pl.BlockSpec((tr, 512), lambda i: (i, COL["iq"] // 512)),
                  pl.BlockSpec((tr, LANES), lambda i: (i, 0)),
                  tab_spec, tab_spec, tab_spec, tab_spec, tab_spec, tab_spec],
        out_specs=(pl.BlockSpec((tr, 1024), lambda i: (i, 0)),
                   pl.BlockSpec((tr, 512), lambda i: (i, 0)),
                   pl.BlockSpec((tr, 256), lambda i: (i, 0)),
                   pl.BlockSpec((tr, I_DH), lambda i: (i, 0))),
        compiler_params=_cparams(("arbitrary",)),
        name="rope",
    )(p_main, p_main, p_main, p_small, *t128, *t64)


def _split3(x):
    hi = x.astype(BF16)
    r1 = x - hi.astype(F32)
    mid = r1.astype(BF16)
    lo = (r1 - mid.astype(F32)).astype(BF16)
    return hi, mid, lo


def _log_sigmoid(x):
    return jnp.minimum(x, 0.0) - jnp.log1p(jnp.exp(-jnp.abs(x)))


def _mlstm_kernel(q_ref, k_ref, v_ref, o_ref, sm_ref, big_r_ref, bfg_r_ref,
                  c0_ref, n0_ref, m0_ref, h_ref, c_ref, n_ref, m_ref, *, L):

    @pl.when(pl.program_id(1) == 0)
    def _():
        c_ref[...] = c0_ref[...]
        n_ref[...] = n0_ref[...]
        m_ref[...] = m0_ref[...]

    g = sm_ref[...]
    ig_c = g[:, SCOL["mi"]:SCOL["mi"] + M_HEADS] + big_r_ref[...]
    lf_c = _log_sigmoid(g[:, SCOL["mf"]:SCOL["mf"] + M_HEADS] + bfg_r_ref[...])

    row = lax.broadcasted_iota(jnp.int32, (L, L), 0)
    col = lax.broadcasted_iota(jnp.int32, (L, L), 1)
    causal = col <= row
    tril = jnp.where(causal, 1.0, 0.0).astype(BF16)
    triu = jnp.where(row <= col, 1.0, 0.0).astype(BF16)
    eye = jnp.where(row == col, 1.0, 0.0).astype(BF16)
    lf_parts = _split3(lf_c)
    b_c = sum(_dot(tril, part) for part in lf_parts)
    b_r = sum(_dot_tn(part, triu) for part in lf_parts)
    ig_r = sum(_dot_tn(part, eye) for part in _split3(ig_c))

    scale = M_DK ** -0.5
    for h in range(M_HEADS):
        sl = slice(h * LANES, (h + 1) * LANES)
        bc = b_c[:, h:h + 1]
        br = b_r[h:h + 1, :]
        igr = ig_r[h:h + 1, :]
        igc = ig_c[:, h:h + 1]
        m_prev = m_ref[0, h:h + 1, 0:1]
        c_prev = c_ref[0, h]
        n_prev = n_ref[0, h:h + 1, :]

        log_d = jnp.where(causal, bc - br + igr, -jnp.inf)
        m_inter = bc + m_prev
        m_t = jnp.maximum(m_inter, jnp.max(log_d, axis=-1, keepdims=True))
        d = jnp.exp(log_d - m_t)
        inter = jnp.exp(m_inter - m_t)

        qh = q_ref[:, sl]
        kh = k_ref[:, sl] * scale
        vh = v_ref[:, sl]
        qb = qh.astype(BF16)
        kb = kh.astype(BF16)
        s = _dot_nt(qb, kb) * d
        num = _dot(s.astype(BF16), vh.astype(BF16)) + _dot_nt(qb, c_prev.astype(BF16)) * inter
        den = jnp.sum(s, axis=-1, keepdims=True) + inter * jnp.sum(qh * n_prev, axis=-1, keepdims=True)
        hout = num / jnp.maximum(jnp.abs(den), jnp.exp(-m_t))
        gate = 1.0 / (1.0 + jnp.exp(-o_ref[:, sl]))
        h_ref[:, sl] = (hout * gate).astype(h_ref.dtype)

        m_new = m_t[L - 1:L, :]
        b_last = bc[L - 1:L, :]
        w_c = jnp.exp(b_last - bc + igc - m_new)
        decay = jnp.exp(b_last + m_prev - m_new)
        c_ref[0, h] = decay * c_prev + _dot_tn((vh * w_c).astype(BF16), kb)
        n_ref[0, h:h + 1, :] = decay * n_prev + jnp.sum(kh * w_c, axis=0, keepdims=True)
        m_ref[0, h:h + 1, :] = jnp.broadcast_to(m_new, (1, LANES))


def mlstm(p_main, p_small, b_ig, b_fg, c0, n0, m0, batch, seq):
    L = min(seq, MLSTM_CHUNK)
    nc = seq // L
    m = batch * seq
    rowblk = lambda cb: pl.BlockSpec((L, 1024), lambda b, c: (b * nc + c, cb))
    st4 = pl.BlockSpec((1, M_HEADS, M_DV, M_DK), lambda b, c: (b, 0, 0, 0))
    st3 = pl.BlockSpec((1, M_HEADS, LANES), lambda b, c: (b, 0, 0))
    vec_r = pl.BlockSpec((1, M_HEADS), lambda b, c: (0, 0))
    return pl.pallas_call(
        functools.partial(_mlstm_kernel, L=L),
        out_shape=(jax.ShapeDtypeStruct((m, M_HEADS * M_DV), BF16),
                   jax.ShapeDtypeStruct((batch, M_HEADS, M_DV, M_DK), F32),
                   jax.ShapeDtypeStruct((batch, M_HEADS, M_DK), F32),
                   jax.ShapeDtypeStruct((batch, M_HEADS, LANES), F32)),
        grid=(batch, nc),
        in_specs=[rowblk(COL["mq"] // 1024), rowblk(COL["mk"] // 1024), rowblk(COL["mv"] // 1024),
                  rowblk(COL["mo"] // 1024),
                  pl.BlockSpec((L, LANES), lambda b, c: (b * nc + c, 0)),
                  vec_r, vec_r, st4, st3, st3],
        out_specs=(pl.BlockSpec((L, 1024), lambda b, c: (b * nc + c, 0)), st4, st3, st3),
        compiler_params=_cparams(("arbitrary", "arbitrary")),
        name="mlstm",
    )(p_main, p_main, p_main, p_main, p_small,
      b_ig.reshape(1, M_HEADS), b_fg.reshape(1, M_HEADS), c0, n0, m0)


def _col_reduce(x, op, final):
    r = x.reshape(x.shape[0] // SUBLANES, SUBLANES, x.shape[1])
    while r.shape[0] > 1:
        half = r.shape[0] // 2
        s = op(r[:half], r[half:2 * half])
        r = s if r.shape[0] % 2 == 0 else jnp.concatenate([s, r[2 * half:]], axis=0)
    return final(r[0], axis=0, keepdims=True)


def _colsum(x):
    return _col_reduce(x, jnp.add, jnp.sum)


def _colmax(x):
    return _col_reduce(x, jnp.maximum, jnp.max)


def _count0(mask):
    return _colsum(jnp.where(mask, 1.0, 0.0))


def _dsa_body(q_ref, qi_ref, sm_ref, out_ref, kcat_s, vt_s, ki_s, sel_s, *, sc, s_real, nb, tq, base_pos, topk, qb):
    hpg = A_HEADS // A_KV
    lane = lax.broadcasted_iota(jnp.int32, (1, QLANES), 1)
    qpos = base_pos + qb * tq + lane % tq
    chunk_end = (qpos // CHUNK + 1) * CHUNK
    sidx = lax.broadcasted_iota(jnp.int32, (sc, 1), 0)
    admissible = sidx < chunk_end
    row_b = lax.broadcasted_iota(jnp.int32, (QLANES, 1), 0) // tq

    def slot_rows(x, bb):
        return x if nb == 1 else jnp.where(row_b == bb, x, 0.0)

    wit = sm_ref[...].T[SCOL["iw"]:SCOL["iw"] + I_HEADS, :]
    score = None
    for h in range(I_HEADS):
        qh = qi_ref[:, h * I_DH:(h + 1) * I_DH].astype(F32) * (I_DH ** -0.5)
        lg = None
        for bb in range(nb):
            part = _dot_nt(ki_s[bb, 0:sc, :], slot_rows(qh, bb).astype(BF16))
            lg = part if lg is None else lg + part
        term = jnp.maximum(lg, 0.0) * wit[h:h + 1, :]
        score = term if score is None else score + term
    score = score * (I_HEADS ** -0.5) + 0.0
    score = jnp.where(admissible, score, NEG)
    if sc > s_real:
        score = jnp.where(sidx < s_real, score, -jnp.inf)

    bits = pltpu.bitcast(score, jnp.int32)
    key = jnp.where(bits < 0, bits ^ jnp.int32(0x7FFFFFFF), bits)
    kf = float(topk)

    def thr_body(i, ans):
        cand = ans + lax.shift_left(jnp.int32(1), jnp.int32(31) - i)
        ok = _count0(key >= cand) >= kf
        return jnp.where(ok, cand, ans)

    thr = lax.fori_loop(0, 32, thr_body, jnp.full((1, QLANES), jnp.iinfo(jnp.int32).min, jnp.int32))
    gt_mask = key > thr
    eq_mask = key == thr
    need = kf - _count0(gt_mask)
    n_eq = _count0(eq_mask)
    sel_s[0:sc, :] = jnp.where(admissible, jnp.where(key >= thr, 1.0, 0.0), 0.0)

    @pl.when(jnp.max(n_eq - need) > 0.0)
    def _():
        def idx_body(i, j):
            cand = j + lax.shift_left(jnp.int32(1), jnp.int32(12) - i)
            ok = _count0(eq_mask & (sidx < cand)) < need
            return jnp.where(ok, cand, j)

        jstar = lax.fori_loop(0, 13, idx_body, jnp.zeros((1, QLANES), jnp.int32))
        sel = gt_mask | (eq_mask & (sidx <= jstar))
        sel_s[0:sc, :] = jnp.where(admissible, jnp.where(sel, 1.0, 0.0), 0.0)

    mask = sel_s[0:sc, :] > 0.5

    lane4_b = (lax.broadcasted_iota(jnp.int32, (1, hpg * QLANES), 1) % QLANES) // tq
    row4_b = (lax.broadcasted_iota(jnp.int32, (hpg * QLANES, 1), 0) % QLANES) // tq
    for j in range(A_KV):
        q4 = jnp.concatenate([q_ref[:, (j * hpg + g) * A_DH:(j * hpg + g + 1) * A_DH] for g in range(hpg)], axis=0)
        if nb > 1:
            q4f = q4.astype(F32)
            q4 = jnp.concatenate([jnp.where(row4_b == bb, q4f, 0.0).astype(BF16) for bb in range(nb)], axis=1)
        st = _dot_nt(kcat_s[j, 0:sc, :], q4)
        ps, ls = [], []
        for g in range(hpg):
            s = jnp.where(mask, st[:, g * QLANES:(g + 1) * QLANES], NEG)
            mx = _colmax(s)
            p = jnp.exp2(s - mx)
            ls.append(_colsum(p))
            ps.append(p.astype(BF16))
        p4 = jnp.concatenate(ps, axis=1)
        l4 = jnp.concatenate(ls, axis=1)
        o4 = None
        for bb in range(nb):
            ob = _dot(vt_s[j, bb, :, 0:sc], p4)
            o4 = ob if o4 is None else jnp.where(lane4_b == bb, ob, o4)
        o4 = o4 / l4
        for g in range(hpg):
            h = j * hpg + g
            out_ref[:, h * A_DH:(h + 1) * A_DH] = o4[:, g * QLANES:(g + 1) * QLANES].T.astype(out_ref.dtype)


def _dsa_kernel(*refs, tq, nb, s_real, s_pad, s_past, seq, base_pos, topk, classes, steps_per_class):
    if s_past:
        (q_ref, qi_ref, sm_ref, kn_ref, vn_ref, kin_ref, pk_ref, pv_ref, pki_ref,
         out_ref, kcat_s, vt_s, ki_s, sel_s, vtmp_s) = refs
    else:
        (q_ref, qi_ref, sm_ref, kn_ref, vn_ref, kin_ref, out_ref, kcat_s, vt_s, ki_s, sel_s, vtmp_s) = refs
    qb = pl.program_id(1)

    @pl.when(qb == 0)
    def _():
        for bb in range(nb):
            rows = slice(bb * seq, (bb + 1) * seq)
            if s_past:
                ki_s[bb, 0:s_past, :] = pki_ref[bb].astype(BF16)
            ki_s[bb, s_past:s_real, :] = kin_ref[rows, :].astype(BF16)
            if s_pad > s_real:
                ki_s[bb, s_real:s_pad, :] = jnp.zeros((s_pad - s_real, I_DH), BF16)
            for j in range(A_KV):
                cols = slice(j * A_DH, (j + 1) * A_DH)
                lanes = slice(bb * A_DH, (bb + 1) * A_DH)
                if s_past:
                    kcat_s[j, 0:s_past, lanes] = pk_ref[bb, pl.ds(j, s_past, stride=A_KV), :].astype(BF16)
                    vtmp_s[0:s_past, :] = pv_ref[bb, pl.ds(j, s_past, stride=A_KV), :]
                kcat_s[j, s_past:s_real, lanes] = kn_ref[rows, cols].astype(BF16)
                vtmp_s[s_past:s_real, :] = vn_ref[rows, cols]
                if s_pad > s_real:
                    kcat_s[j, s_real:s_pad, lanes] = jnp.zeros((s_pad - s_real, A_DH), BF16)
                    vtmp_s[s_real:s_pad, :] = jnp.zeros((s_pad - s_real, A_DH), F32)
                vt_s[j, bb] = vtmp_s[...].T.astype(BF16)

    body = functools.partial(_dsa_body, q_ref, qi_ref, sm_ref, out_ref, kcat_s, vt_s, ki_s, sel_s,
                             s_real=s_real, nb=nb, tq=tq, base_pos=base_pos, topk=topk, qb=qb)
    if len(classes) == 1:
        body(sc=classes[0])
    else:
        for c, sc in enumerate(classes):
            pl.when(qb // steps_per_class == c)(functools.partial(body, sc=sc))


def dsa(q_r, qi_r, p_small, k_new, v_src, v_colblk, ki_new, past, batch, seq, base_pos):
    m = batch * seq
    tq = min(seq, QLANES)
    nb = QLANES // tq
    nq = seq // tq
    s_past = 0 if past is None else past[2].shape[1]
    s_real = s_past + seq
    s_pad = -(-s_real // LANES) * LANES
    topk = min(TOPK_MAX, s_real // 4)
    assert topk <= s_real and s_pad < 8192 and batch % nb == 0
    n_cls = 4 if (past is None and nb == 1 and nq % 4 == 0 and s_pad % (4 * LANES) == 0) else 1
    classes = tuple(s_pad * (c + 1) // n_cls for c in range(n_cls))
    assert n_cls == 1 or base_pos == 0
    qrow = lambda w: pl.BlockSpec((QLANES, w), lambda b, i: (b * nq + i, 0))
    in_specs = [qrow(1024), qrow(512), qrow(LANES),
                pl.BlockSpec((nb * seq, 256), lambda b, i: (b, 0)),
                pl.BlockSpec((nb * seq, 256), lambda b, i: (b, v_colblk)),
                pl.BlockSpec((nb * seq, I_DH), lambda b, i: (b, 0))]
    args = [q_r, qi_r, p_small, k_new, v_src, ki_new]
    if past is not None:
        in_specs += [pl.BlockSpec((nb, s_past * A_KV, A_DH), lambda b, i: (b, 0, 0)),
                     pl.BlockSpec((nb, s_past * A_KV, A_DH), lambda b, i: (b, 0, 0)),
                     pl.BlockSpec((nb, s_past, I_DH), lambda b, i: (b, 0, 0))]
        args += list(past)
    kern = functools.partial(_dsa_kernel, tq=tq, nb=nb, s_real=s_real, s_pad=s_pad, s_past=s_past, seq=seq,
                             base_pos=base_pos, topk=topk, classes=classes, steps_per_class=nq // n_cls)
    return pl.pallas_call(
        kern,
        out_shape=jax.ShapeDtypeStruct((m, A_HEADS * A_DH), BF16),
        grid=(batch // nb, nq),
        in_specs=in_specs,
        out_specs=pl.BlockSpec((QLANES, 1024), lambda b, i: (b * nq + i, 0)),
        scratch_shapes=[pltpu.VMEM((A_KV, s_pad, nb * A_DH), BF16),
                        pltpu.VMEM((A_KV, nb, A_DH, s_pad), BF16),
                        pltpu.VMEM((nb, s_pad, I_DH), BF16),
                        pltpu.VMEM((s_pad, QLANES), F32),
                        pltpu.VMEM((s_pad, A_DH), F32)],
        compiler_params=_cparams(("arbitrary", "arbitrary")),
        name="dsa",
    )(*args)


def _mem_attn_kernel(q_ref, k_ref, v_ref, o_ref, *, rows_by_head):
    for h in range(C_HEADS):
        sl = slice(h * C_DH, (h + 1) * C_DH)
        if rows_by_head:
            parts = C_DH // LANES
            step = C_HEADS * parts
            kh = jnp.concatenate([k_ref[pl.ds(c * C_HEADS + h, N_MEM, stride=step), :] for c in range(parts)], axis=1)
            vh = jnp.concatenate([v_ref[pl.ds(c * C_HEADS + h, N_MEM, stride=step), :] for c in range(parts)], axis=1)
        else:
            kh, vh = k_ref[:, sl], v_ref[:, sl]
        s = _dot_nt(q_ref[:, sl].astype(BF16), kh.astype(BF16)) * (C_DH ** -0.5)
        mx = jnp.max(s, axis=-1, keepdims=True)
        p = jnp.exp(s - mx)
        l = jnp.sum(p, axis=-1, keepdims=True)
        o_ref[:, sl] = (_dot(p.astype(BF16), vh.astype(BF16)) / l).astype(o_ref.dtype)


def _mem_rows(cache):
    b = cache.shape[0]
    parts = C_DH // LANES
    x = cache.reshape(b, N_MEM, C_HEADS, parts, LANES).transpose(0, 1, 3, 2, 4)
    return x.reshape(b * N_MEM * parts * C_HEADS, LANES)


def mem_attn(p_main, mk, mk_colblk, mv, mv_colblk, batch, seq, tq):
    m = batch * seq
    nq = seq // tq
    w = C_HEADS * C_DH
    rows_by_head = mk_colblk is None
    if rows_by_head:
        kv_specs = [pl.BlockSpec((N_MEM * w // LANES, LANES), lambda b, i: (b, 0))] * 2
    else:
        kv_specs = [pl.BlockSpec((N_MEM, w), lambda b, i: (b, mk_colblk)),
                    pl.BlockSpec((N_MEM, w), lambda b, i: (b, mv_colblk))]
    return pl.pallas_call(
        functools.partial(_mem_attn_kernel, rows_by_head=rows_by_head),
        out_shape=jax.ShapeDtypeStruct((m, w), BF16),
        grid=(batch, nq),
        in_specs=[pl.BlockSpec((tq, w), lambda b, i: (b * nq + i, COL["cq"] // w))] + kv_specs,
        out_specs=pl.BlockSpec((tq, w), lambda b, i: (b * nq + i, 0)),
        compiler_params=_cparams(("arbitrary", "arbitrary")),
        name="mem_attn",
    )(p_main, mk, mv)


def _merge_kernel(a_ref, b_ref, c_ref, ga_ref, gb_ref, gc_ref, w_ref, o_ref):
    acc = None
    for br_ref, g_ref, n in ((a_ref, ga_ref, 0), (b_ref, gb_ref, 1), (c_ref, gc_ref, 2)):
        proj = _dot(br_ref[...], w_ref[n])
        term = proj / (1.0 + jnp.exp(-g_ref[...]))
        acc = term if acc is None else acc + term
    o_ref[...] = acc.astype(o_ref.dtype)


def merge(br_a, br_b, br_c, p_main, w_branch, tm, tn):
    m = br_a.shape[0]
    nj = D_MODEL // tn
    brs = pl.BlockSpec((tm, BRANCH_W), lambda i, j: (i, 0))
    gate = lambda n: pl.BlockSpec((tm, tn), lambda i, j: (i, (COL["gt"] + n * D_MODEL) // tn + j))
    return pl.pallas_call(
        _merge_kernel,
        out_shape=jax.ShapeDtypeStruct((m, D_MODEL), BF16),
        grid=(m // tm, nj),
        in_specs=[brs, brs, brs, gate(0), gate(1), gate(2),
                  pl.BlockSpec((N_BRANCH, BRANCH_W, tn), lambda i, j: (0, 0, j))],
        out_specs=pl.BlockSpec((tm, tn), lambda i, j: (i, j)),
        compiler_params=_cparams(("arbitrary", "arbitrary")),
        name="merge",
    )(br_a, br_b, br_c, p_main, p_main, p_main, w_branch)


def _ffn_kernel(h_ref, g_ref, wg_ref, wv_ref, cwg_ref, cwv_ref, cbg_ref, cbv_ref, pg_ref, pv_ref, wd_ref, gf_ref,
                y_ref, tg_ref, tv_ref, xn_ref, ug_s, uv_s, cg_s, cv_s, *, tm, rb, seg, blocks_per_seq):
    i = pl.program_id(0)
    j = pl.program_id(1)
    hdr = SUBLANES
    sub = min(seg, rb)

    @pl.when(j == 0)
    def _():
        x = h_ref[...]
        ms = jnp.mean(x * x, axis=-1, keepdims=True)
        xn_ref[...] = (x * lax.rsqrt(ms + EPS) * g_ref[...]).astype(BF16)
        y_ref[...] = jnp.zeros_like(y_ref)

    if blocks_per_seq > 1:
        @pl.when((i % blocks_per_seq) == 0)
        def _():
            cg_s[j] = pg_ref[0]
            cv_s[j] = pv_ref[0]

    def conv(u, prev, cw_ref, cb_ref, u_s):
        u_s[hdr - 2:hdr, :] = prev
        u_s[hdr:hdr + sub, :] = u
        return (cb_ref[...] + u_s[hdr - 2:hdr - 2 + sub, :] * cw_ref[0:1, :]
                + u_s[hdr - 1:hdr - 1 + sub, :] * cw_ref[1:2, :] + u * cw_ref[2:3, :])

    tail_g = tail_v = None
    for r in range(tm // rb):
        xn = xn_ref[r * rb:(r + 1) * rb, :]
        ug = _dot(xn, wg_ref[...])
        uv = _dot(xn, wv_ref[...])
        acts = []
        for s in range(rb // sub):
            row0 = r * rb + s * sub
            sq = row0 // seg
            rows = slice(s * sub, (s + 1) * sub)
            ugs, uvs = ug[rows, :], uv[rows, :]
            if row0 % seg == 0:
                prev_g = cg_s[j] if blocks_per_seq > 1 else pg_ref[sq]
                prev_v = cv_s[j] if blocks_per_seq > 1 else pv_ref[sq]
            else:
                prev_g, prev_v = tail_g, tail_v
            cg = conv(ugs, prev_g, cwg_ref, cbg_ref, ug_s.at[r])
            cv = conv(uvs, prev_v, cwv_ref, cbv_ref, uv_s.at[r])
            acts.append((cg / (1.0 + jnp.exp(-cg)) * cv).astype(BF16))
            tail_g, tail_v = ugs[sub - 2:sub, :], uvs[sub - 2:sub, :]
            if (row0 + sub) % seg == 0:
                tg_ref[sq] = tail_g
                tv_ref[sq] = tail_v
                if blocks_per_seq > 1:
                    cg_s[j] = tail_g
                    cv_s[j] = tail_v
        act = acts[0] if len(acts) == 1 else jnp.concatenate(acts, axis=0)
        cw = 512
        for c in range(D_MODEL // cw):
            cols = slice(c * cw, (c + 1) * cw)
            y_ref[r * rb:(r + 1) * rb, cols] += _dot(act, wd_ref[:, cols])

    @pl.when(j == pl.num_programs(1) - 1)
    def _():
        z = h_ref[...] + y_ref[...]
        ms = jnp.mean(z * z, axis=-1, keepdims=True)
        y_ref[...] = z * lax.rsqrt(ms + EPS) * gf_ref[...]


def ffn(h, g_ffn, w_up, conv_w, conv_b, conv_prev, w_down, g_final, seq, tm, tn):
    m = h.shape[0]
    seg = min(tm, seq)
    nseg = tm // seg
    bps = seq // seg
    blocks_per_seq = bps if nseg == 1 else 1
    nj = D_FF // tn
    prev_spec = lambda off: pl.BlockSpec((nseg, CONV_W - 1, tn),
                                         lambda i, j: ((i // blocks_per_seq) if nseg == 1 else i, 0, off + j))
    tail_spec = pl.BlockSpec((nseg, CONV_W - 1, tn), lambda i, j: (i, 0, j))
    rb = min(tm, 512)
    sub = min(seg, rb)
    kern = functools.partial(_ffn_kernel, tm=tm, rb=rb, seg=seg, blocks_per_seq=blocks_per_seq)
    return pl.pallas_call(
        kern,
        out_shape=(jax.ShapeDtypeStruct((m, D_MODEL), F32),
                   jax.ShapeDtypeStruct((m // seg, CONV_W - 1, D_FF), F32),
                   jax.ShapeDtypeStruct((m // seg, CONV_W - 1, D_FF), F32)),
        grid=(m // tm, nj),
        in_specs=[pl.BlockSpec((tm, D_MODEL), lambda i, j: (i, 0), pipeline_mode=pl.Buffered(1)),
                  pl.BlockSpec((1, D_MODEL), lambda i, j: (0, 0)),
                  pl.BlockSpec((D_MODEL, tn), lambda i, j: (0, j)),
                  pl.BlockSpec((D_MODEL, tn), lambda i, j: (0, nj + j)),
                  pl.BlockSpec((CONV_W, tn), lambda i, j: (0, j)),
                  pl.BlockSpec((CONV_W, tn), lambda i, j: (0, nj + j)),
                  pl.BlockSpec((1, tn), lambda i, j: (0, j)),
                  pl.BlockSpec((1, tn), lambda i, j: (0, nj + j)),
                  prev_spec(0), prev_spec(nj),
                  pl.BlockSpec((tn, D_MODEL), lambda i, j: (j, 0)),
                  pl.BlockSpec((1, D_MODEL), lambda i, j: (0, 0))],
        out_specs=(pl.BlockSpec((tm, D_MODEL), lambda i, j: (i, 0), pipeline_mode=pl.Buffered(1)),
                   tail_spec, tail_spec),
        scratch_shapes=[pltpu.VMEM((tm, D_MODEL), BF16),
                        pltpu.VMEM((tm // rb, SUBLANES + sub, tn), F32),
                        pltpu.VMEM((tm // rb, SUBLANES + sub, tn), F32),
                        pltpu.VMEM((nj, CONV_W - 1, tn), F32), pltpu.VMEM((nj, CONV_W - 1, tn), F32)],
        compiler_params=pltpu.CompilerParams(dimension_semantics=("arbitrary", "arbitrary"),
                                             vmem_limit_bytes=FFN_VMEM_LIMIT),
        name="ffn",
    )(h, g_ffn.reshape(1, D_MODEL), w_up, w_up, conv_w, conv_w,
      conv_b.reshape(1, 2 * D_FF), conv_b.reshape(1, 2 * D_FF), conv_prev, conv_prev,
      w_down, g_final.reshape(1, D_MODEL))


def _group(x, pos, past, mstate, mem_k, mem_kblk, mem_v, mem_vblk, conv_prev, wts, base_pos):
    (g_mix, w_main, w_small, b_ig, b_fg, w_branch, w_out, g_ffn, w_up, conv_w, conv_b, w_down, g_final) = wts
    batch, seq, _ = x.shape
    m = batch * seq
    x2 = x.reshape(m, D_MODEL)

    p_main = norm_matmul(x2, g_mix, w_main, 1024, 1024, "in_proj")
    p_small = norm_matmul(x2, g_mix, w_small, 1024, LANES, "in_proj_small")

    q_r, qi_r, k_new, ki_new = rope(p_main, p_small, pos, seq, 512)
    v_new = p_main[:, COL["av"]:COL["av"] + A_KV * A_DH]

    c0, n0, m0 = mstate
    br_a, c_new, n_new, m_new = mlstm(p_main, p_small, b_ig, b_fg, c0, n0,
                                      jnp.broadcast_to(m0[:, :, None], (batch, M_HEADS, LANES)), batch, seq)
    br_b = dsa(q_r, qi_r, p_small, k_new, p_main, COL["av"] // 256, ki_new, past, batch, seq, base_pos)
    br_c = mem_attn(p_main, mem_k, mem_kblk, mem_v, mem_vblk, batch, seq, min(seq, 512))

    merged = merge(br_a, br_b, br_c, p_main, w_branch, 1024, 512)
    h = matmul_res(merged, w_out, x2, 1024, 1024, "out_proj")

    y, tail_g, tail_v = ffn(h, g_ffn, w_up, conv_w, conv_b, conv_prev, w_down, g_final, seq, 1024, 512)

    nblk = tail_g.shape[0] // batch
    conv_new = jnp.concatenate([tail_g.reshape(batch, nblk, CONV_W - 1, D_FF)[:, -1],
                                tail_v.reshape(batch, nblk, CONV_W - 1, D_FF)[:, -1]], axis=-1)
    outs = (y.reshape(batch, seq, D_MODEL),
            k_new.reshape(batch, seq, A_KV, A_DH), v_new.reshape(batch, seq, A_KV, A_DH),
            ki_new.reshape(batch, seq, I_DH), c_new, n_new, m_new[:, :, 0], conv_new)
    return outs


def kernel(x_prompt, x_sample, mem_prompt, cache_dsa_k, cache_dsa_v, cache_dsa_kidx, state_mlstm_C, state_mlstm_n, state_mlstm_m, state_ffn_conv, cache_mem_k, cache_mem_v, g_mix, w_in, b_igate, b_fgate, w_branch, w_out, g_mem, w_mem_kv, g_ffn, w_up, conv_w, conv_b, w_down, g_final):
    depth = w_in.shape[0]
    assert depth == 1
    B, T = x_prompt.shape[:2]
    DB, DT = x_sample.shape[:2]
    P = cache_dsa_k.shape[2]
    l = 0

    w = w_in[l]
    segs = {}
    off = 0
    for name, width in zip(ORIG_NAMES, ORIG_SPLITS):
        segs[name] = w[:, off:off + width]
        off += width
    w_main = jnp.concatenate([segs[n] for n in MAIN_ORDER], axis=1).astype(BF16)
    small_w = sum(ORIG_SPLITS[ORIG_NAMES.index(n)] for n in SMALL_ORDER)
    w_small = jnp.concatenate([segs[n] for n in SMALL_ORDER] + [jnp.zeros((D_MODEL, LANES - small_w), F32)],
                              axis=1).astype(BF16)
    wts = (g_mix[l], w_main, w_small, b_igate[l], b_fgate[l], w_branch[l].astype(BF16), w_out[l].astype(BF16),
           g_ffn[l], w_up[l].astype(BF16), conv_w[l], conv_b[l], w_down[l].astype(BF16), g_final)

    mkv = norm_matmul(mem_prompt.reshape(B * N_MEM, D_MODEL), g_mem[l], w_mem_kv[l].astype(BF16), 512, 1024, "mem_kv")
    zero_state = (jnp.zeros((B, M_HEADS, M_DV, M_DK), F32), jnp.zeros((B, M_HEADS, M_DK), F32),
                  jnp.zeros((B, M_HEADS), F32))
    zero_conv = jnp.zeros((B, CONV_W - 1, 2 * D_FF), F32)
    (y_p, k_p, v_p, ki_p, c_p, n_p, m_p, conv_p) = _group(
        x_prompt, jnp.arange(T, dtype=jnp.int32), None, zero_state, mkv, 0, mkv, 1, zero_conv, wts, 0)
    hw = C_HEADS * C_DH
    mk_p = mkv[:, :hw].reshape(B, N_MEM, C_HEADS, C_DH)
    mv_p = mkv[:, hw:].reshape(B, N_MEM, C_HEADS, C_DH)

    past = (cache_dsa_k[l].reshape(DB, P * A_KV, A_DH), cache_dsa_v[l].reshape(DB, P * A_KV, A_DH),
            cache_dsa_kidx[l])
    mstate = (state_mlstm_C[l], state_mlstm_n[l], state_mlstm_m[l])
    (y_s, k_s, v_s, ki_s, c_s, n_s, m_s, conv_s) = _group(
        x_sample, P + jnp.arange(DT, dtype=jnp.int32), past, mstate,
        _mem_rows(cache_mem_k[l]), None, _mem_rows(cache_mem_v[l]), None,
        state_ffn_conv[l], wts, P)

    st = lambda a: a[None]
    return (y_p, y_s,
            st(k_p), st(v_p), st(ki_p), st(c_p), st(n_p), st(m_p), st(conv_p), st(mk_p), st(mv_p),
            st(k_s), st(v_s), st(ki_s), st(c_s), st(n_s), st(m_s), st(conv_s))
```

```python
import functools
import math

import jax
import jax.numpy as jnp
from jax import lax
from jax.experimental import pallas as pl
from jax.experimental.pallas import tpu as pltpu

F32 = jnp.float32
BF16 = jnp.bfloat16

D_MODEL = 2048
CHUNK = 64
MLSTM_CHUNK = 256
M_HEADS, M_DK, M_DV = 8, 128, 128
A_HEADS, A_KV, A_DH = 8, 2, 128
I_HEADS, I_DH = 8, 64
TOPK_MAX = 256
N_MEM, C_HEADS, C_DH = 256, 4, 256
N_BRANCH, BRANCH_W = 3, 1024
D_FF = 5632
CONV_W = 3
ROPE_THETA = 500000.0
ROPE_FRAC = 4
EPS = 1e-6
NEG = -1e30

LANES = 128
SUBLANES = 8
VMEM_LIMIT = 48 * 1024 * 1024
FFN_VMEM_LIMIT = 56 * 1024 * 1024

QLANES = LANES
Q_SCALE = A_DH ** -0.5 * math.log2(math.e)

ORIG_SPLITS = (1024, 1024, 1024, 1024, 8, 8, 1024, 256, 256, 512, 64, 8, 1024, 6144)
ORIG_NAMES = ("mq", "mk", "mv", "mo", "mi", "mf", "aq", "ak", "av", "iq", "ik", "iw", "cq", "gt")
MAIN_ORDER = ("mq", "mk", "mv", "mo", "aq", "cq", "gt", "ak", "av", "iq")
SMALL_ORDER = ("ik", "mi", "mf", "iw")
MAIN_W = 13312
COL = {}
_off = 0
for _n in MAIN_ORDER:
    COL[_n] = _off
    _off += ORIG_SPLITS[ORIG_NAMES.index(_n)]
assert _off == MAIN_W
SCOL = {"ik": 0, "mi": 64, "mf": 72, "iw": 80}


def _cparams(sem):
    return pltpu.CompilerParams(dimension_semantics=sem, vmem_limit_bytes=VMEM_LIMIT)


def _dot(a, b):
    return jnp.dot(a, b, preferred_element_type=F32)


def _dot_nt(a, b):
    return lax.dot_general(a, b, (((1,), (1,)), ((), ())), preferred_element_type=F32)


def _dot_tn(a, b):
    return lax.dot_general(a, b, (((0,), (0,)), ((), ())), preferred_element_type=F32)


def _norm_matmul_kernel(x_ref, g_ref, w_ref, o_ref, xn_ref):
    @pl.when(pl.program_id(1) == 0)
    def _():
        x = x_ref[...]
        ms = jnp.mean(x * x, axis=-1, keepdims=True)
        xn_ref[...] = (x * lax.rsqrt(ms + EPS) * g_ref[...]).astype(BF16)

    o_ref[...] = _dot(xn_ref[...], w_ref[...]).astype(o_ref.dtype)


def norm_matmul(x, g, w, tm, tn, name, out_dtype=F32):
    m, k = x.shape
    n = w.shape[1]
    return pl.pallas_call(
        _norm_matmul_kernel,
        out_shape=jax.ShapeDtypeStruct((m, n), out_dtype),
        grid=(m // tm, n // tn),
        in_specs=[pl.BlockSpec((tm, k), lambda i, j: (i, 0)),
                  pl.BlockSpec((1, k), lambda i, j: (0, 0)),
                  pl.BlockSpec((k, tn), lambda i, j: (0, j))],
        out_specs=pl.BlockSpec((tm, tn), lambda i, j: (i, j)),
        scratch_shapes=[pltpu.VMEM((tm, k), BF16)],
        compiler_params=_cparams(("arbitrary", "arbitrary")),
        name=name,
    )(x, g.reshape(1, k), w)


def _matmul_res_kernel(a_ref, w_ref, r_ref, o_ref):
    o_ref[...] = _dot(a_ref[...], w_ref[...]) + r_ref[...]


def matmul_res(a, w, res, tm, tn, name):
    m, k = a.shape
    n = w.shape[1]
    return pl.pallas_call(
        _matmul_res_kernel,
        out_shape=jax.ShapeDtypeStruct((m, n), F32),
        grid=(m // tm, n // tn),
        in_specs=[pl.BlockSpec((tm, k), lambda i, j: (i, 0)),
                  pl.BlockSpec((k, tn), lambda i, j: (0, j)),
                  pl.BlockSpec((tm, tn), lambda i, j: (i, j))],
        out_specs=pl.BlockSpec((tm, tn), lambda i, j: (i, j)),
        compiler_params=_cparams(("arbitrary", "arbitrary")),
        name=name,
    )(a, w, res)


def _rope_tables(pos, dh):
    r = dh // ROPE_FRAC
    half = r // 2
    inv = ROPE_THETA ** (-jnp.arange(half, dtype=F32) * 2.0 / r)
    ang = pos.astype(F32)[:, None] * inv[None, :]
    cos, sin = jnp.cos(ang), jnp.sin(ang)
    t = pos.shape[0]
    ones = jnp.ones((t, dh - r), F32)
    zeros_h = jnp.zeros((t, half), F32)
    zeros_rest = jnp.zeros((t, dh - r), F32)
    c = jnp.concatenate([cos, cos, ones], axis=1)
    sa = jnp.concatenate([zeros_h, sin, zeros_rest], axis=1)
    sb = jnp.concatenate([-sin, zeros_h, zeros_rest], axis=1)
    rep = LANES // dh
    return tuple(jnp.tile(a, (1, rep)) for a in (c, sa, sb))


def _rope_apply(x, c, sa, sb, half):
    return x * c + pltpu.roll(x, half, 1) * sa + pltpu.roll(x, LANES - half, 1) * sb


def _rope_kernel(aq_ref, ak_ref, iq_ref, sm_ref, c128_ref, sa128_ref, sb128_ref, c64_ref, sa64_ref, sb64_ref,
                 q_ref, qi_ref, k_ref, ki_ref):
    c1, a1, b1 = c128_ref[...], sa128_ref[...], sb128_ref[...]
    c2, a2, b2 = c64_ref[...], sa64_ref[...], sb64_ref[...]
    h1 = A_DH // ROPE_FRAC // 2
    h2 = I_DH // ROPE_FRAC // 2
    for h in range(A_HEADS):
        sl = slice(h * LANES, (h + 1) * LANES)
        q_ref[:, sl] = (_rope_apply(aq_ref[:, sl].astype(F32), c1, a1, b1, h1) * Q_SCALE).astype(q_ref.dtype)
    for h in range(A_KV):
        sl = slice(h * LANES, (h + 1) * LANES)
        k_ref[:, sl] = _rope_apply(ak_ref[:, sl].astype(F32), c1, a1, b1, h1)
    for h in range(I_HEADS * I_DH // LANES):
        sl = slice(h * LANES, (h + 1) * LANES)
        qi_ref[:, sl] = _rope_apply(iq_ref[:, sl].astype(F32), c2, a2, b2, h2).astype(qi_ref.dtype)
    ki = _rope_apply(sm_ref[...], c2, a2, b2, h2)
    ki_ref[...] = ki[:, :I_DH]


def rope(p_main, p_small, pos, seq, tr):
    m = p_main.shape[0]
    t128 = _rope_tables(pos, A_DH)
    t64 = _rope_tables(pos, I_DH)
    if seq < tr:
        t128 = tuple(jnp.tile(a, (tr // seq, 1)) for a in t128)
        t64 = tuple(jnp.tile(a, (tr // seq, 1)) for a in t64)
    ntab = max(seq, tr) // tr
    tab_spec = pl.BlockSpec((tr, LANES), lambda i: (i % ntab, 0))
    return pl.pallas_call(
        _rope_kernel,
        out_shape=(jax.ShapeDtypeStruct((m, A_HEADS * A_DH), BF16),
                   jax.ShapeDtypeStruct((m, I_HEADS * I_DH), BF16),
                   jax.ShapeDtypeStruct((m, A_KV * A_DH), F32),
                   jax.ShapeDtypeStruct((m, I_DH), F32)),
        grid=(m // tr,),
        in_specs=[pl.BlockSpec((tr, 1024), lambda i: (i, COL["aq"] // 1024)),
                  pl.BlockSpec((tr, 256), lambda i: (i, COL["ak"] // 256)),
                  pl.BlockSpec((tr, 512), lambda i: (i, COL["iq"] // 512)),
                  pl.BlockSpec((tr, LANES), lambda i: (i, 0)),
                  tab_spec, tab_spec, tab_spec, tab_spec, tab_spec, tab_spec],
        out_specs=(pl.BlockSpec((tr, 1024), lambda i: (i, 0)),
                   pl.BlockSpec((tr, 512), lambda i: (i, 0)),
                   pl.BlockSpec((tr, 256), lambda i: (i, 0)),
                   pl.BlockSpec((tr, I_DH), lambda i: (i, 0))),
        compiler_params=_cparams(("arbitrary",)),
        name="rope",
    )(p_main, p_main, p_main, p_small, *t128, *t64)


def _split3(x):
    hi = x.astype(BF16)
    r1 = x - hi.astype(F32)
    mid = r1.astype(BF16)
    lo = (r1 - mid.astype(F32)).astype(BF16)
    return hi, mid, lo


def _log_sigmoid(x):
    return jnp.minimum(x, 0.0) - jnp.log1p(jnp.exp(-jnp.abs(x)))


def _mlstm_kernel(q_ref, k_ref, v_ref, o_ref, sm_ref, big_r_ref, bfg_r_ref,
                  c0_ref, n0_ref, m0_ref, h_ref, c_ref, n_ref, m_ref, *, L):

    @pl.when(pl.program_id(1) == 0)
    def _():
        c_ref[...] = c0_ref[...]
        n_ref[...] = n0_ref[...]
        m_ref[...] = m0_ref[...]

    g = sm_ref[...]
    ig_c = g[:, SCOL["mi"]:SCOL["mi"] + M_HEADS] + big_r_ref[...]
    lf_c = _log_sigmoid(g[:, SCOL["mf"]:SCOL["mf"] + M_HEADS] + bfg_r_ref[...])

    row = lax.broadcasted_iota(jnp.int32, (L, L), 0)
    col = lax.broadcasted_iota(jnp.int32, (L, L), 1)
    causal = col <= row
    tril = jnp.where(causal, 1.0, 0.0).astype(BF16)
    triu = jnp.where(row <= col, 1.0, 0.0).astype(BF16)
    eye = jnp.where(row == col, 1.0, 0.0).astype(BF16)
    lf_parts = _split3(lf_c)
    b_c = sum(_dot(tril, part) for part in lf_parts)
    b_r = sum(_dot_tn(part, triu) for part in lf_parts)
    ig_r = sum(_dot_tn(part, eye) for part in _split3(ig_c))

    scale = M_DK ** -0.5
    for h in range(M_HEADS):
        sl = slice(h * LANES, (h + 1) * LANES)
        bc = b_c[:, h:h + 1]
        br = b_r[h:h + 1, :]
        igr = ig_r[h:h + 1, :]
        igc = ig_c[:, h:h + 1]
        m_prev = m_ref[0, h:h + 1, 0:1]
        c_prev = c_ref[0, h]
        n_prev = n_ref[0, h:h + 1, :]

        log_d = jnp.where(causal, bc - br + igr, -jnp.inf)
        m_inter = bc + m_prev
        m_t = jnp.maximum(m_inter, jnp.max(log_d, axis=-1, keepdims=True))
        d = jnp.exp(log_d - m_t)
        inter = jnp.exp(m_inter - m_t)

        qh = q_ref[:, sl].astype(F32)
        kh = k_ref[:, sl].astype(F32) * scale
        vh = v_ref[:, sl].astype(F32)
        qb = qh.astype(BF16)
        kb = kh.astype(BF16)
        s = _dot_nt(qb, kb) * d
        num = _dot(s.astype(BF16), vh.astype(BF16)) + _dot_nt(qb, c_prev.astype(BF16)) * inter
        den = jnp.sum(s, axis=-1, keepdims=True) + inter * jnp.sum(qh * n_prev, axis=-1, keepdims=True)
        hout = num / jnp.maximum(jnp.abs(den), jnp.exp(-m_t))
        gate = 1.0 / (1.0 + jnp.exp(-o_ref[:, sl].astype(F32)))
        h_ref[:, sl] = (hout * gate).astype(h_ref.dtype)

        m_new = m_t[L - 1:L, :]
        b_last = bc[L - 1:L, :]
        w_c = jnp.exp(b_last - bc + igc - m_new)
        decay = jnp.exp(b_last + m_prev - m_new)
        c_ref[0, h] = decay * c_prev + _dot_tn((vh * w_c).astype(BF16), kb)
        n_ref[0, h:h + 1, :] = decay * n_prev + jnp.sum(kh * w_c, axis=0, keepdims=True)
        m_ref[0, h:h + 1, :] = jnp.broadcast_to(m_new, (1, LANES))


def mlstm(p_main, p_small, b_ig, b_fg, c0, n0, m0, batch, seq):
    L = min(seq, MLSTM_CHUNK)
    nc = seq // L
    m = batch * seq
    rowblk = lambda cb: pl.BlockSpec((L, 1024), lambda b, c: (b * nc + c, cb))
    st4 = pl.BlockSpec((1, M_HEADS, M_DV, M_DK), lambda b, c: (b, 0, 0, 0))
    st3 = pl.BlockSpec((1, M_HEADS, LANES), lambda b, c: (b, 0, 0))
    vec_r = pl.BlockSpec((1, M_HEADS), lambda b, c: (0, 0))
    return pl.pallas_call(
        functools.partial(_mlstm_kernel, L=L),
        out_shape=(jax.ShapeDtypeStruct((m, M_HEADS * M_DV), BF16),
                   jax.ShapeDtypeStruct((batch, M_HEADS, M_DV, M_DK), F32),
                   jax.ShapeDtypeStruct((batch, M_HEADS, M_DK), F32),
                   jax.ShapeDtypeStruct((batch, M_HEADS, LANES), F32)),
        grid=(batch, nc),
        in_specs=[rowblk(COL["mq"] // 1024), rowblk(COL["mk"] // 1024), rowblk(COL["mv"] // 1024),
                  rowblk(COL["mo"] // 1024),
                  pl.BlockSpec((L, LANES), lambda b, c: (b * nc + c, 0)),
                  vec_r, vec_r, st4, st3, st3],
        out_specs=(pl.BlockSpec((L, 1024), lambda b, c: (b * nc + c, 0)), st4, st3, st3),
        compiler_params=_cparams(("arbitrary", "arbitrary")),
        name="mlstm",
    )(p_main, p_main, p_main, p_main, p_small,
      b_ig.reshape(1, M_HEADS), b_fg.reshape(1, M_HEADS), c0, n0, m0)


def _col_reduce(x, op, final):
    r = x.reshape(x.shape[0] // SUBLANES, SUBLANES, x.shape[1])
    while r.shape[0] > 1:
        half = r.shape[0] // 2
        s = op(r[:half], r[half:2 * half])
        r = s if r.shape[0] % 2 == 0 else jnp.concatenate([s, r[2 * half:]], axis=0)
    return final(r[0], axis=0, keepdims=True)


def _colsum(x):
    return _col_reduce(x, jnp.add, jnp.sum)


def _colmax(x):
    return _col_reduce(x, jnp.maximum, jnp.max)


def _count0(mask):
    return _colsum(jnp.where(mask, 1.0, 0.0))


def _dsa_body(q_ref, qi_ref, sm_ref, out_ref, kcat_s, vt_s, ki_s, sel_s, *, sc, s_real, nb, tq, base_pos, topk, qb):
    hpg = A_HEADS // A_KV
    lane = lax.broadcasted_iota(jnp.int32, (1, QLANES), 1)
    qpos = base_pos + qb * tq + lane % tq
    chunk_end = (qpos // CHUNK + 1) * CHUNK
    sidx = lax.broadcasted_iota(jnp.int32, (sc, 1), 0)
    admissible = sidx < chunk_end
    row_b = lax.broadcasted_iota(jnp.int32, (QLANES, 1), 0) // tq

    def slot_rows(x, bb):
        return x if nb == 1 else jnp.where(row_b == bb, x, 0.0)

    wit = sm_ref[...].T[SCOL["iw"]:SCOL["iw"] + I_HEADS, :]
    score = None
    for h in range(I_HEADS):
        qh = qi_ref[:, h * I_DH:(h + 1) * I_DH].astype(F32) * (I_DH ** -0.5)
        lg = None
        for bb in range(nb):
            part = _dot_nt(ki_s[bb, 0:sc, :], slot_rows(qh, bb).astype(BF16))
            lg = part if lg is None else lg + part
        term = jnp.maximum(lg, 0.0) * wit[h:h + 1, :]
        score = term if score is None else score + term
    score = score * (I_HEADS ** -0.5) + 0.0
    score = jnp.where(admissible, score, NEG)
    if sc > s_real:
        score = jnp.where(sidx < s_real, score, -jnp.inf)

    kf = float(topk)

    def key_to_float(key):
        return pltpu.bitcast(jnp.where(key < 0, key ^ jnp.int32(0x7FFFFFFF), key), F32)

    def thr_body(i, ans):
        cand = ans + lax.shift_left(jnp.int32(1), jnp.int32(31) - i)
        ok = _count0(score >= key_to_float(cand)) >= kf
        return jnp.where(ok, cand, ans)

    thr_key = lax.fori_loop(0, 32, thr_body, jnp.full((1, QLANES), jnp.iinfo(jnp.int32).min, jnp.int32))
    thr = key_to_float(thr_key)
    gt_mask = score > thr
    eq_mask = score == thr
    need = kf - _count0(gt_mask)
    n_eq = _count0(eq_mask)
    sel_s[0:sc, :] = jnp.where(admissible, jnp.where(score >= thr, 1.0, 0.0), 0.0)

    @pl.when(jnp.max(n_eq - need) > 0.0)
    def _():
        def idx_body(i, j):
            cand = j + lax.shift_left(jnp.int32(1), jnp.int32(12) - i)
            ok = _count0(eq_mask & (sidx < cand)) < need
            return jnp.where(ok, cand, j)

        jstar = lax.fori_loop(0, 13, idx_body, jnp.zeros((1, QLANES), jnp.int32))
        sel = gt_mask | (eq_mask & (sidx <= jstar))
        sel_s[0:sc, :] = jnp.where(admissible, jnp.where(sel, 1.0, 0.0), 0.0)

    mask = sel_s[0:sc, :] > 0.5

    lane4_b = (lax.broadcasted_iota(jnp.int32, (1, hpg * QLANES), 1) % QLANES) // tq
    row4_b = (lax.broadcasted_iota(jnp.int32, (hpg * QLANES, 1), 0) % QLANES) // tq
    for j in range(A_KV):
        q4 = jnp.concatenate([q_ref[:, (j * hpg + g) * A_DH:(j * hpg + g + 1) * A_DH] for g in range(hpg)], axis=0)
        if nb > 1:
            q4f = q4.astype(F32)
            q4 = jnp.concatenate([jnp.where(row4_b == bb, q4f, 0.0).astype(BF16) for bb in range(nb)], axis=1)
        st = _dot_nt(kcat_s[j, 0:sc, :], q4)
        ps, ls = [], []
        for g in range(hpg):
            s = jnp.where(mask, st[:, g * QLANES:(g + 1) * QLANES], NEG)
            mx = _colmax(s)
            p = jnp.exp2(s - mx)
            ls.append(_colsum(p))
            ps.append(p.astype(BF16))
        p4 = jnp.concatenate(ps, axis=1)
        l4 = jnp.concatenate(ls, axis=1)
        o4 = None
        for bb in range(nb):
            ob = _dot(vt_s[j, bb, :, 0:sc], p4)
            o4 = ob if o4 is None else jnp.where(lane4_b == bb, ob, o4)
        o4 = o4 / l4
        for g in range(hpg):
            h = j * hpg + g
            out_ref[:, h * A_DH:(h + 1) * A_DH] = o4[:, g * QLANES:(g + 1) * QLANES].T.astype(out_ref.dtype)


def _dsa_kernel(*refs, tq, nb, s_real, s_pad, s_past, seq, base_pos, topk, classes, steps_per_class):
    if s_past:
        (q_ref, qi_ref, sm_ref, kn_ref, vn_ref, kin_ref, pk_ref, pv_ref, pki_ref,
         out_ref, kcat_s, vt_s, ki_s, sel_s, vtmp_s) = refs
    else:
        (q_ref, qi_ref, sm_ref, kn_ref, vn_ref, kin_ref, out_ref, kcat_s, vt_s, ki_s, sel_s, vtmp_s) = refs
    qb = pl.program_id(1)

    @pl.when(qb == 0)
    def _():
        for bb in range(nb):
            rows = slice(bb * seq, (bb + 1) * seq)
            if s_past:
                ki_s[bb, 0:s_past, :] = pki_ref[bb].astype(BF16)
            ki_s[bb, s_past:s_real, :] = kin_ref[rows, :].astype(BF16)
            if s_pad > s_real:
                ki_s[bb, s_real:s_pad, :] = jnp.zeros((s_pad - s_real, I_DH), BF16)
            for j in range(A_KV):
                cols = slice(j * A_DH, (j + 1) * A_DH)
                lanes = slice(bb * A_DH, (bb + 1) * A_DH)
                if s_past:
                    kcat_s[j, 0:s_past, lanes] = pk_ref[bb, pl.ds(j, s_past, stride=A_KV), :].astype(BF16)
                    vtmp_s[0:s_past, :] = pv_ref[bb, pl.ds(j, s_past, stride=A_KV), :]
                kcat_s[j, s_past:s_real, lanes] = kn_ref[rows, cols].astype(BF16)
                vtmp_s[s_past:s_real, :] = vn_ref[rows, cols].astype(F32)
                if s_pad > s_real:
                    kcat_s[j, s_real:s_pad, lanes] = jnp.zeros((s_pad - s_real, A_DH), BF16)
                    vtmp_s[s_real:s_pad, :] = jnp.zeros((s_pad - s_real, A_DH), F32)
                vt_s[j, bb] = vtmp_s[...].T.astype(BF16)

    body = functools.partial(_dsa_body, q_ref, qi_ref, sm_ref, out_ref, kcat_s, vt_s, ki_s, sel_s,
                             s_real=s_real, nb=nb, tq=tq, base_pos=base_pos, topk=topk, qb=qb)
    if len(classes) == 1:
        body(sc=classes[0])
    else:
        for c, sc in enumerate(classes):
            pl.when(qb // steps_per_class == c)(functools.partial(body, sc=sc))


def dsa(q_r, qi_r, p_small, k_new, v_src, v_colblk, ki_new, past, batch, seq, base_pos):
    m = batch * seq
    tq = min(seq, QLANES)
    nb = QLANES // tq
    nq = seq // tq
    s_past = 0 if past is None else past[2].shape[1]
    s_real = s_past + seq
    s_pad = -(-s_real // LANES) * LANES
    topk = min(TOPK_MAX, s_real // 4)
    assert topk <= s_real and s_pad < 8192 and batch % nb == 0
    n_cls = 4 if (past is None and nb == 1 and nq % 4 == 0 and s_pad % (4 * LANES) == 0) else 1
    classes = tuple(s_pad * (c + 1) // n_cls for c in range(n_cls))
    assert n_cls == 1 or base_pos == 0
    qrow = lambda w: pl.BlockSpec((QLANES, w), lambda b, i: (b * nq + i, 0))
    in_specs = [qrow(1024), qrow(512), qrow(LANES),
                pl.BlockSpec((nb * seq, 256), lambda b, i: (b, 0)),
                pl.BlockSpec((nb * seq, 256), lambda b, i: (b, v_colblk)),
                pl.BlockSpec((nb * seq, I_DH), lambda b, i: (b, 0))]
    args = [q_r, qi_r, p_small, k_new, v_src, ki_new]
    if past is not None:
        in_specs += [pl.BlockSpec((nb, s_past * A_KV, A_DH), lambda b, i: (b, 0, 0)),
                     pl.BlockSpec((nb, s_past * A_KV, A_DH), lambda b, i: (b, 0, 0)),
                     pl.BlockSpec((nb, s_past, I_DH), lambda b, i: (b, 0, 0))]
        args += list(past)
    kern = functools.partial(_dsa_kernel, tq=tq, nb=nb, s_real=s_real, s_pad=s_pad, s_past=s_past, seq=seq,
                             base_pos=base_pos, topk=topk, classes=classes, steps_per_class=nq // n_cls)
    return pl.pallas_call(
        kern,
        out_shape=jax.ShapeDtypeStruct((m, A_HEADS * A_DH), BF16),
        grid=(batch // nb, nq),
        in_specs=in_specs,
        out_specs=pl.BlockSpec((QLANES, 1024), lambda b, i: (b * nq + i, 0)),
        scratch_shapes=[pltpu.VMEM((A_KV, s_pad, nb * A_DH), BF16),
                        pltpu.VMEM((A_KV, nb, A_DH, s_pad), BF16),
                        pltpu.VMEM((nb, s_pad, I_DH), BF16),
                        pltpu.VMEM((s_pad, QLANES), F32),
                        pltpu.VMEM((s_pad, A_DH), F32)],
        compiler_params=_cparams(("arbitrary", "arbitrary")),
        name="dsa",
    )(*args)


def _mem_attn_kernel(q_ref, k_ref, v_ref, o_ref, *, rows_by_head):
    for h in range(C_HEADS):
        sl = slice(h * C_DH, (h + 1) * C_DH)
        if rows_by_head:
            parts = C_DH // LANES
            step = C_HEADS * parts
            kh = jnp.concatenate([k_ref[pl.ds(c * C_HEADS + h, N_MEM, stride=step), :] for c in range(parts)], axis=1)
            vh = jnp.concatenate([v_ref[pl.ds(c * C_HEADS + h, N_MEM, stride=step), :] for c in range(parts)], axis=1)
        else:
            kh, vh = k_ref[:, sl], v_ref[:, sl]
        s = _dot_nt(q_ref[:, sl].astype(BF16), kh.astype(BF16)) * (C_DH ** -0.5)
        mx = jnp.max(s, axis=-1, keepdims=True)
        p = jnp.exp(s - mx)
        l = jnp.sum(p, axis=-1, keepdims=True)
        o_ref[:, sl] = (_dot(p.astype(BF16), vh.astype(BF16)) / l).astype(o_ref.dtype)


def _mem_rows(cache):
    b = cache.shape[0]
    parts = C_DH // LANES
    x = cache.reshape(b, N_MEM, C_HEADS, parts, LANES).transpose(0, 1, 3, 2, 4)
    return x.reshape(b * N_MEM * parts * C_HEADS, LANES)


def mem_attn(p_main, mk, mk_colblk, mv, mv_colblk, batch, seq, tq):
    m = batch * seq
    nq = seq // tq
    w = C_HEADS * C_DH
    rows_by_head = mk_colblk is None
    if rows_by_head:
        kv_specs = [pl.BlockSpec((N_MEM * w // LANES, LANES), lambda b, i: (b, 0))] * 2
    else:
        kv_specs = [pl.BlockSpec((N_MEM, w), lambda b, i: (b, mk_colblk)),
                    pl.BlockSpec((N_MEM, w), lambda b, i: (b, mv_colblk))]
    return pl.pallas_call(
        functools.partial(_mem_attn_kernel, rows_by_head=rows_by_head),
        out_shape=jax.ShapeDtypeStruct((m, w), BF16),
        grid=(batch, nq),
        in_specs=[pl.BlockSpec((tq, w), lambda b, i: (b * nq + i, COL["cq"] // w))] + kv_specs,
        out_specs=pl.BlockSpec((tq, w), lambda b, i: (b * nq + i, 0)),
        compiler_params=_cparams(("arbitrary", "arbitrary")),
        name="mem_attn",
    )(p_main, mk, mv)


def _merge_kernel(a_ref, b_ref, c_ref, ga_ref, gb_ref, gc_ref, w_ref, o_ref):
    acc = None
    for br_ref, g_ref, n in ((a_ref, ga_ref, 0), (b_ref, gb_ref, 1), (c_ref, gc_ref, 2)):
        proj = _dot(br_ref[...], w_ref[n])
        term = proj / (1.0 + jnp.exp(-g_ref[...].astype(F32)))
        acc = term if acc is None else acc + term
    o_ref[...] = acc.astype(o_ref.dtype)


def merge(br_a, br_b, br_c, p_main, w_branch, tm, tn):
    m = br_a.shape[0]
    nj = D_MODEL // tn
    brs = pl.BlockSpec((tm, BRANCH_W), lambda i, j: (i, 0))
    gate = lambda n: pl.BlockSpec((tm, tn), lambda i, j: (i, (COL["gt"] + n * D_MODEL) // tn + j))
    return pl.pallas_call(
        _merge_kernel,
        out_shape=jax.ShapeDtypeStruct((m, D_MODEL), BF16),
        grid=(m // tm, nj),
        in_specs=[brs, brs, brs, gate(0), gate(1), gate(2),
                  pl.BlockSpec((N_BRANCH, BRANCH_W, tn), lambda i, j: (0, 0, j))],
        out_specs=pl.BlockSpec((tm, tn), lambda i, j: (i, j)),
        compiler_params=_cparams(("arbitrary", "arbitrary")),
        name="merge",
    )(br_a, br_b, br_c, p_main, p_main, p_main, w_branch)


def _ffn_kernel(h_ref, g_ref, wg_ref, wv_ref, cwg_ref, cwv_ref, cbg_ref, cbv_ref, pg_ref, pv_ref, wd_ref, gf_ref,
                y_ref, tg_ref, tv_ref, xn_ref, ug_s, uv_s, cg_s, cv_s, *, tm, rb, seg, blocks_per_seq):
    i = pl.program_id(0)
    j = pl.program_id(1)
    hdr = SUBLANES
    sub = min(seg, rb)

    @pl.when(j == 0)
    def _():
        x = h_ref[...]
        ms = jnp.mean(x * x, axis=-1, keepdims=True)
        xn_ref[...] = (x * lax.rsqrt(ms + EPS) * g_ref[...]).astype(BF16)
        y_ref[...] = jnp.zeros_like(y_ref)

    if blocks_per_seq > 1:
        @pl.when((i % blocks_per_seq) == 0)
        def _():
            cg_s[j] = pg_ref[0]
            cv_s[j] = pv_ref[0]

    def conv(u, prev, cw_ref, cb_ref, u_s):
        u_s[hdr - 2:hdr, :] = prev
        u_s[hdr:hdr + sub, :] = u
        return (cb_ref[...] + u_s[hdr - 2:hdr - 2 + sub, :] * cw_ref[0:1, :]
                + u_s[hdr - 1:hdr - 1 + sub, :] * cw_ref[1:2, :] + u * cw_ref[2:3, :])

    tail_g = tail_v = None
    for r in range(tm // rb):
        xn = xn_ref[r * rb:(r + 1) * rb, :]
        ug = _dot(xn, wg_ref[...])
        uv = _dot(xn, wv_ref[...])
        acts = []
        for s in range(rb // sub):
            row0 = r * rb + s * sub
            sq = row0 // seg
            rows = slice(s * sub, (s + 1) * sub)
            ugs, uvs = ug[rows, :], uv[rows, :]
            if row0 % seg == 0:
                prev_g = cg_s[j] if blocks_per_seq > 1 else pg_ref[sq]
                prev_v = cv_s[j] if blocks_per_seq > 1 else pv_ref[sq]
            else:
                prev_g, prev_v = tail_g, tail_v
            cg = conv(ugs, prev_g, cwg_ref, cbg_ref, ug_s.at[r])
            cv = conv(uvs, prev_v, cwv_ref, cbv_ref, uv_s.at[r])
            acts.append((cg / (1.0 + jnp.exp(-cg)) * cv).astype(BF16))
            tail_g, tail_v = ugs[sub - 2:sub, :], uvs[sub - 2:sub, :]
            if (row0 + sub) % seg == 0:
                tg_ref[sq] = tail_g
                tv_ref[sq] = tail_v
                if blocks_per_seq > 1:
                    cg_s[j] = tail_g
                    cv_s[j] = tail_v
        act = acts[0] if len(acts) == 1 else jnp.concatenate(acts, axis=0)
        cw = 512
        for c in range(D_MODEL // cw):
            cols = slice(c * cw, (c + 1) * cw)
            y_ref[r * rb:(r + 1) * rb, cols] += _dot(act, wd_ref[:, cols])

    @pl.when(j == pl.num_programs(1) - 1)
    def _():
        z = h_ref[...] + y_ref[...]
        ms = jnp.mean(z * z, axis=-1, keepdims=True)
        y_ref[...] = z * lax.rsqrt(ms + EPS) * gf_ref[...]


def ffn(h, g_ffn, w_up, conv_w, conv_b, conv_prev, w_down, g_final, seq, tm, tn):
    m = h.shape[0]
    seg = min(tm, seq)
    nseg = tm // seg
    bps = seq // seg
    blocks_per_seq = bps if nseg == 1 else 1
    nj = D_FF // tn
    prev_spec = lambda off: pl.BlockSpec((nseg, CONV_W - 1, tn),
                                         lambda i, j: ((i // blocks_per_seq) if nseg == 1 else i, 0, off + j))
    tail_spec = pl.BlockSpec((nseg, CONV_W - 1, tn), lambda i, j: (i, 0, j))
    rb = min(tm, 512)
    sub = min(seg, rb)
    kern = functools.partial(_ffn_kernel, tm=tm, rb=rb, seg=seg, blocks_per_seq=blocks_per_seq)
    return pl.pallas_call(
        kern,
        out_shape=(jax.ShapeDtypeStruct((m, D_MODEL), F32),
                   jax.ShapeDtypeStruct((m // seg, CONV_W - 1, D_FF), F32),
                   jax.ShapeDtypeStruct((m // seg, CONV_W - 1, D_FF), F32)),
        grid=(m // tm, nj),
        in_specs=[pl.BlockSpec((tm, D_MODEL), lambda i, j: (i, 0), pipeline_mode=pl.Buffered(1)),
                  pl.BlockSpec((1, D_MODEL), lambda i, j: (0, 0)),
                  pl.BlockSpec((D_MODEL, tn), lambda i, j: (0, j)),
                  pl.BlockSpec((D_MODEL, tn), lambda i, j: (0, nj + j)),
                  pl.BlockSpec((CONV_W, tn), lambda i, j: (0, j)),
                  pl.BlockSpec((CONV_W, tn), lambda i, j: (0, nj + j)),
                  pl.BlockSpec((1, tn), lambda i, j: (0, j)),
                  pl.BlockSpec((1, tn), lambda i, j: (0, nj + j)),
                  prev_spec(0), prev_spec(nj),
                  pl.BlockSpec((tn, D_MODEL), lambda i, j: (j, 0)),
                  pl.BlockSpec((1, D_MODEL), lambda i, j: (0, 0))],
        out_specs=(pl.BlockSpec((tm, D_MODEL), lambda i, j: (i, 0), pipeline_mode=pl.Buffered(1)),
                   tail_spec, tail_spec),
        scratch_shapes=[pltpu.VMEM((tm, D_MODEL), BF16),
                        pltpu.VMEM((tm // rb, SUBLANES + sub, tn), F32),
                        pltpu.VMEM((tm // rb, SUBLANES + sub, tn), F32),
                        pltpu.VMEM((nj, CONV_W - 1, tn), F32), pltpu.VMEM((nj, CONV_W - 1, tn), F32)],
        compiler_params=pltpu.CompilerParams(dimension_semantics=("arbitrary", "arbitrary"),
                                             vmem_limit_bytes=FFN_VMEM_LIMIT),
        name="ffn",
    )(h, g_ffn.reshape(1, D_MODEL), w_up, w_up, conv_w, conv_w,
      conv_b.reshape(1, 2 * D_FF), conv_b.reshape(1, 2 * D_FF), conv_prev, conv_prev,
      w_down, g_final.reshape(1, D_MODEL))


def _group(x, pos, past, mstate, mem_k, mem_kblk, mem_v, mem_vblk, conv_prev, wts, base_pos):
    (g_mix, w_main, w_small, b_ig, b_fg, w_branch, w_out, g_ffn, w_up, conv_w, conv_b, w_down, g_final) = wts
    batch, seq, _ = x.shape
    m = batch * seq
    x2 = x.reshape(m, D_MODEL)

    p_main = norm_matmul(x2, g_mix, w_main, 1024, 1024, "in_proj", out_dtype=BF16)
    p_small = norm_matmul(x2, g_mix, w_small, 1024, LANES, "in_proj_small")

    q_r, qi_r, k_new, ki_new = rope(p_main, p_small, pos, seq, 512)
    v_new = p_main[:, COL["av"]:COL["av"] + A_KV * A_DH].astype(F32)

    c0, n0, m0 = mstate
    br_a, c_new, n_new, m_new = mlstm(p_main, p_small, b_ig, b_fg, c0, n0,
                                      jnp.broadcast_to(m0[:, :, None], (batch, M_HEADS, LANES)), batch, seq)
    br_b = dsa(q_r, qi_r, p_small, k_new, p_main, COL["av"] // 256, ki_new, past, batch, seq, base_pos)
    br_c = mem_attn(p_main, mem_k, mem_kblk, mem_v, mem_vblk, batch, seq, min(seq, 512))

    merged = merge(br_a, br_b, br_c, p_main, w_branch, 1024, 512)
    h = matmul_res(merged, w_out, x2, 1024, 1024, "out_proj")

    y, tail_g, tail_v = ffn(h, g_ffn, w_up, conv_w, conv_b, conv_prev, w_down, g_final, seq, 1024, 512)

    nblk = tail_g.shape[0] // batch
    conv_new = jnp.concatenate([tail_g.reshape(batch, nblk, CONV_W - 1, D_FF)[:, -1],
                                tail_v.reshape(batch, nblk, CONV_W - 1, D_FF)[:, -1]], axis=-1)
    outs = (y.reshape(batch, seq, D_MODEL),
            k_new.reshape(batch, seq, A_KV, A_DH), v_new.reshape(batch, seq, A_KV, A_DH),
            ki_new.reshape(batch, seq, I_DH), c_new, n_new, m_new[:, :, 0], conv_new)
    return outs


def kernel(x_prompt, x_sample, mem_prompt, cache_dsa_k, cache_dsa_v, cache_dsa_kidx, state_mlstm_C, state_mlstm_n, state_mlstm_m, state_ffn_conv, cache_mem_k, cache_mem_v, g_mix, w_in, b_igate, b_fgate, w_branch, w_out, g_mem, w_mem_kv, g_ffn, w_up, conv_w, conv_b, w_down, g_final):
    depth = w_in.shape[0]
    assert depth == 1
    B, T = x_prompt.shape[:2]
    DB, DT = x_sample.shape[:2]
    P = cache_dsa_k.shape[2]
    l = 0

    w = w_in[l]
    segs = {}
    off = 0
    for name, width in zip(ORIG_NAMES, ORIG_SPLITS):
        segs[name] = w[:, off:off + width]
        off += width
    w_main = jnp.concatenate([segs[n] for n in MAIN_ORDER], axis=1).astype(BF16)
    small_w = sum(ORIG_SPLITS[ORIG_NAMES.index(n)] for n in SMALL_ORDER)
    w_small = jnp.concatenate([segs[n] for n in SMALL_ORDER] + [jnp.zeros((D_MODEL, LANES - small_w), F32)],
                              axis=1).astype(BF16)
    wts = (g_mix[l], w_main, w_small, b_igate[l], b_fgate[l], w_branch[l].astype(BF16), w_out[l].astype(BF16),
           g_ffn[l], w_up[l].astype(BF16), conv_w[l], conv_b[l], w_down[l].astype(BF16), g_final)

    mkv = norm_matmul(mem_prompt.reshape(B * N_MEM, D_MODEL), g_mem[l], w_mem_kv[l].astype(BF16), 512, 1024, "mem_kv")
    zero_state = (jnp.zeros((B, M_HEADS, M_DV, M_DK), F32), jnp.zeros((B, M_HEADS, M_DK), F32),
                  jnp.zeros((B, M_HEADS), F32))
    zero_conv = jnp.zeros((B, CONV_W - 1, 2 * D_FF), F32)
    (y_p, k_p, v_p, ki_p, c_p, n_p, m_p, conv_p) = _group(
        x_prompt, jnp.arange(T, dtype=jnp.int32), None, zero_state, mkv, 0, mkv, 1, zero_conv, wts, 0)
    hw = C_HEADS * C_DH
    mk_p = mkv[:, :hw].reshape(B, N_MEM, C_HEADS, C_DH)
    mv_p = mkv[:, hw:].reshape(B, N_MEM, C_HEADS, C_DH)

    past = (cache_dsa_k[l].reshape(DB, P * A_KV, A_DH), cache_dsa_v[l].reshape(DB, P * A_KV, A_DH),
            cache_dsa_kidx[l])
    mstate = (state_mlstm_C[l], state_mlstm_n[l], state_mlstm_m[l])
    (y_s, k_s, v_s, ki_s, c_s, n_s, m_s, conv_s) = _group(
        x_sample, P + jnp.arange(DT, dtype=jnp.int32), past, mstate,
        _mem_rows(cache_mem_k[l]), None, _mem_rows(cache_mem_v[l]), None,
        state_ffn_conv[l], wts, P)

    st = lambda a: a[None]
    return (y_p, y_s,
            st(k_p), st(v_p), st(ki_p), st(c_p), st(n_p), st(m_p), st(conv_p), st(mk_p), st(mv_p),
            st(k_s), st(v_s), st(ki_s), st(c_s), st(n_s), st(m_s), st(conv_s))
```

```python
import functools
import math

import jax
import jax.numpy as jnp
from jax import lax
from jax.experimental import pallas as pl
from jax.experimental.pallas import tpu as pltpu

F32 = jnp.float32
BF16 = jnp.bfloat16

D_MODEL = 2048
CHUNK = 64
MLSTM_CHUNK = 256
M_HEADS, M_DK, M_DV = 8, 128, 128
A_HEADS, A_KV, A_DH = 8, 2, 128
I_HEADS, I_DH = 8, 64
TOPK_MAX = 256
N_MEM, C_HEADS, C_DH = 256, 4, 256
N_BRANCH, BRANCH_W = 3, 1024
D_FF = 5632
CONV_W = 3
ROPE_THETA = 500000.0
ROPE_FRAC = 4
EPS = 1e-6
NEG = -1e30

LANES = 128
SUBLANES = 8
VMEM_LIMIT = 48 * 1024 * 1024
FFN_VMEM_LIMIT = 56 * 1024 * 1024

QLANES = LANES
Q_SCALE = A_DH ** -0.5 * math.log2(math.e)

ORIG_SPLITS = (1024, 1024, 1024, 1024, 8, 8, 1024, 256, 256, 512, 64, 8, 1024, 6144)
ORIG_NAMES = ("mq", "mk", "mv", "mo", "mi", "mf", "aq", "ak", "av", "iq", "ik", "iw", "cq", "gt")
MAIN_ORDER = ("mq", "mk", "mv", "mo", "aq", "cq", "gt", "ak", "av", "iq")
SMALL_ORDER = ("ik", "mi", "mf", "iw")
MAIN_W = 13312
COL = {}
_off = 0
for _n in MAIN_ORDER:
    COL[_n] = _off
    _off += ORIG_SPLITS[ORIG_NAMES.index(_n)]
assert _off == MAIN_W
SCOL = {"ik": 0, "mi": 64, "mf": 72, "iw": 80}


def _cparams(sem):
    return pltpu.CompilerParams(dimension_semantics=sem, vmem_limit_bytes=VMEM_LIMIT)


def _dot(a, b):
    return jnp.dot(a, b, preferred_element_type=F32)


def _dot_nt(a, b):
    return lax.dot_general(a, b, (((1,), (1,)), ((), ())), preferred_element_type=F32)


def _dot_tn(a, b):
    return lax.dot_general(a, b, (((0,), (0,)), ((), ())), preferred_element_type=F32)


def _norm_matmul_kernel(x_ref, g_ref, w_ref, o_ref, xn_ref):
    @pl.when(pl.program_id(1) == 0)
    def _():
        x = x_ref[...]
        ms = jnp.mean(x * x, axis=-1, keepdims=True)
        xn_ref[...] = (x * lax.rsqrt(ms + EPS) * g_ref[...]).astype(BF16)

    o_ref[...] = _dot(xn_ref[...], w_ref[...]).astype(o_ref.dtype)


def norm_matmul(x, g, w, tm, tn, name, out_dtype=F32):
    m, k = x.shape
    n = w.shape[1]
    return pl.pallas_call(
        _norm_matmul_kernel,
        out_shape=jax.ShapeDtypeStruct((m, n), out_dtype),
        grid=(m // tm, n // tn),
        in_specs=[pl.BlockSpec((tm, k), lambda i, j: (i, 0)),
                  pl.BlockSpec((1, k), lambda i, j: (0, 0)),
                  pl.BlockSpec((k, tn), lambda i, j: (0, j))],
        out_specs=pl.BlockSpec((tm, tn), lambda i, j: (i, j)),
        scratch_shapes=[pltpu.VMEM((tm, k), BF16)],
        compiler_params=_cparams(("arbitrary", "arbitrary")),
        name=name,
    )(x, g.reshape(1, k), w)


def _matmul_res_kernel(a_ref, w_ref, r_ref, o_ref):
    o_ref[...] = _dot(a_ref[...], w_ref[...]) + r_ref[...]


def matmul_res(a, w, res, tm, tn, name):
    m, k = a.shape
    n = w.shape[1]
    return pl.pallas_call(
        _matmul_res_kernel,
        out_shape=jax.ShapeDtypeStruct((m, n), F32),
        grid=(m // tm, n // tn),
        in_specs=[pl.BlockSpec((tm, k), lambda i, j: (i, 0)),
                  pl.BlockSpec((k, tn), lambda i, j: (0, j)),
                  pl.BlockSpec((tm, tn), lambda i, j: (i, j))],
        out_specs=pl.BlockSpec((tm, tn), lambda i, j: (i, j)),
        compiler_params=_cparams(("arbitrary", "arbitrary")),
        name=name,
    )(a, w, res)


def _rope_tables(pos, dh):
    r = dh // ROPE_FRAC
    half = r // 2
    inv = ROPE_THETA ** (-jnp.arange(half, dtype=F32) * 2.0 / r)
    ang = pos.astype(F32)[:, None] * inv[None, :]
    cos, sin = jnp.cos(ang), jnp.sin(ang)
    t = pos.shape[0]
    ones = jnp.ones((t, dh - r), F32)
    zeros_h = jnp.zeros((t, half), F32)
    zeros_rest = jnp.zeros((t, dh - r), F32)
    c = jnp.concatenate([cos, cos, ones], axis=1)
    sa = jnp.concatenate([zeros_h, sin, zeros_rest], axis=1)
    sb = jnp.concatenate([-sin, zeros_h, zeros_rest], axis=1)
    rep = LANES // dh
    return tuple(jnp.tile(a, (1, rep)) for a in (c, sa, sb))


def _rope_apply(x, c, sa, sb, half):
    return x * c + pltpu.roll(x, half, 1) * sa + pltpu.roll(x, LANES - half, 1) * sb


def _rope_kernel(aq_ref, ak_ref, iq_ref, sm_ref, c128_ref, sa128_ref, sb128_ref, c64_ref, sa64_ref, sb64_ref,
                 q_ref, qi_ref, k_ref, ki_ref):
    c1, a1, b1 = c128_ref[...], sa128_ref[...], sb128_ref[...]
    c2, a2, b2 = c64_ref[...], sa64_ref[...], sb64_ref[...]
    h1 = A_DH // ROPE_FRAC // 2
    h2 = I_DH // ROPE_FRAC // 2
    for h in range(A_HEADS):
        sl = slice(h * LANES, (h + 1) * LANES)
        q_ref[:, sl] = (_rope_apply(aq_ref[:, sl].astype(F32), c1, a1, b1, h1) * Q_SCALE).astype(q_ref.dtype)
    for h in range(A_KV):
        sl = slice(h * LANES, (h + 1) * LANES)
        k_ref[:, sl] = _rope_apply(ak_ref[:, sl].astype(F32), c1, a1, b1, h1)
    for h in range(I_HEADS * I_DH // LANES):
        sl = slice(h * LANES, (h + 1) * LANES)
        qi_ref[:, sl] = _rope_apply(iq_ref[:, sl].astype(F32), c2, a2, b2, h2).astype(qi_ref.dtype)
    ki = _rope_apply(sm_ref[...], c2, a2, b2, h2)
    ki_ref[...] = ki[:, :I_DH]


def rope(p_main, p_small, pos, seq, tr):
    m = p_main.shape[0]
    t128 = _rope_tables(pos, A_DH)
    t64 = _rope_tables(pos, I_DH)
    if seq < tr:
        t128 = tuple(jnp.tile(a, (tr // seq, 1)) for a in t128)
        t64 = tuple(jnp.tile(a, (tr // seq, 1)) for a in t64)
    ntab = max(seq, tr) // tr
    tab_spec = pl.BlockSpec((tr, LANES), lambda i: (i % ntab, 0))
    return pl.pallas_call(
        _rope_kernel,
        out_shape=(jax.ShapeDtypeStruct((m, A_HEADS * A_DH), BF16),
                   jax.ShapeDtypeStruct((m, I_HEADS * I_DH), BF16),
                   jax.ShapeDtypeStruct((m, A_KV * A_DH), F32),
                   jax.ShapeDtypeStruct((m, I_DH), F32)),
        grid=(m // tr,),
        in_specs=[pl.BlockSpec((tr, 1024), lambda i: (i, COL["aq"] // 1024)),
                  pl.BlockSpec((tr, 256), lambda i: (i, COL["ak"] // 256)),
                  pl.BlockSpec((tr, 512), lambda i: (i, COL["iq"] // 512)),
                  pl.BlockSpec((tr, LANES), lambda i: (i, 0)),
                  tab_spec, tab_spec, tab_spec, tab_spec, tab_spec, tab_spec],
        out_specs=(pl.BlockSpec((tr, 1024), lambda i: (i, 0)),
                   pl.BlockSpec((tr, 512), lambda i: (i, 0)),
                   pl.BlockSpec((tr, 256), lambda i: (i, 0)),
                   pl.BlockSpec((tr, I_DH), lambda i: (i, 0))),
        compiler_params=_cparams(("arbitrary",)),
        name="rope",
    )(p_main, p_main, p_main, p_small, *t128, *t64)


def _split3(x):
    hi = x.astype(BF16)
    r1 = x - hi.astype(F32)
    mid = r1.astype(BF16)
    lo = (r1 - mid.astype(F32)).astype(BF16)
    return hi, mid, lo


def _log_sigmoid(x):
    return jnp.minimum(x, 0.0) - jnp.log1p(jnp.exp(-jnp.abs(x)))


def _mlstm_kernel(q_ref, k_ref, v_ref, o_ref, sm_ref, big_r_ref, bfg_r_ref,
                  c0_ref, n0_ref, m0_ref, h_ref, c_ref, n_ref, m_ref, *, L):

    @pl.when(pl.program_id(1) == 0)
    def _():
        c_ref[...] = c0_ref[...]
        n_ref[...] = n0_ref[...]
        m_ref[...] = m0_ref[...]

    g = sm_ref[...]
    ig_c = g[:, SCOL["mi"]:SCOL["mi"] + M_HEADS] + big_r_ref[...]
    lf_c = _log_sigmoid(g[:, SCOL["mf"]:SCOL["mf"] + M_HEADS] + bfg_r_ref[...])

    row = lax.broadcasted_iota(jnp.int32, (L, L), 0)
    col = lax.broadcasted_iota(jnp.int32, (L, L), 1)
    causal = col <= row
    tril = jnp.where(causal, 1.0, 0.0).astype(BF16)
    triu = jnp.where(row <= col, 1.0, 0.0).astype(BF16)
    eye = jnp.where(row == col, 1.0, 0.0).astype(BF16)
    lf_parts = _split3(lf_c)
    b_c = sum(_dot(tril, part) for part in lf_parts)
    b_r = sum(_dot_tn(part, triu) for part in lf_parts)
    ig_r = sum(_dot_tn(part, eye) for part in _split3(ig_c))

    scale = M_DK ** -0.5
    for h in range(M_HEADS):
        sl = slice(h * LANES, (h + 1) * LANES)
        bc = b_c[:, h:h + 1]
        br = b_r[h:h + 1, :]
        igr = ig_r[h:h + 1, :]
        igc = ig_c[:, h:h + 1]
        m_prev = m_ref[0, h:h + 1, 0:1]
        c_prev = c_ref[0, h]
        n_prev = n_ref[0, h:h + 1, :]

        log_d = jnp.where(causal, bc - br + igr, -jnp.inf)
        m_inter = bc + m_prev
        m_t = jnp.maximum(m_inter, jnp.max(log_d, axis=-1, keepdims=True))
        d = jnp.exp(log_d - m_t)
        inter = jnp.exp(m_inter - m_t)

        qh = q_ref[:, sl].astype(F32)
        kh = k_ref[:, sl].astype(F32) * scale
        vh = v_ref[:, sl].astype(F32)
        qb = qh.astype(BF16)
        kb = kh.astype(BF16)
        s = _dot_nt(qb, kb) * d
        num = _dot(s.astype(BF16), vh.astype(BF16)) + _dot_nt(qb, c_prev.astype(BF16)) * inter
        den = jnp.sum(s, axis=-1, keepdims=True) + inter * jnp.sum(qh * n_prev, axis=-1, keepdims=True)
        hout = num / jnp.maximum(jnp.abs(den), jnp.exp(-m_t))
        gate = 1.0 / (1.0 + jnp.exp(-o_ref[:, sl].astype(F32)))
        h_ref[:, sl] = (hout * gate).astype(h_ref.dtype)

        m_new = m_t[L - 1:L, :]
        b_last = bc[L - 1:L, :]
        w_c = jnp.exp(b_last - bc + igc - m_new)
        decay = jnp.exp(b_last + m_prev - m_new)
        c_ref[0, h] = decay * c_prev + _dot_tn((vh * w_c).astype(BF16), kb)
        n_ref[0, h:h + 1, :] = decay * n_prev + jnp.sum(kh * w_c, axis=0, keepdims=True)
        m_ref[0, h:h + 1, :] = jnp.broadcast_to(m_new, (1, LANES))


def mlstm(p_main, p_small, b_ig, b_fg, c0, n0, m0, batch, seq):
    L = min(seq, MLSTM_CHUNK)
    nc = seq // L
    m = batch * seq
    rowblk = lambda cb: pl.BlockSpec((L, 1024), lambda b, c: (b * nc + c, cb))
    st4 = pl.BlockSpec((1, M_HEADS, M_DV, M_DK), lambda b, c: (b, 0, 0, 0))
    st3 = pl.BlockSpec((1, M_HEADS, LANES), lambda b, c: (b, 0, 0))
    vec_r = pl.BlockSpec((1, M_HEADS), lambda b, c: (0, 0))
    return pl.pallas_call(
        functools.partial(_mlstm_kernel, L=L),
        out_shape=(jax.ShapeDtypeStruct((m, M_HEADS * M_DV), BF16),
                   jax.ShapeDtypeStruct((batch, M_HEADS, M_DV, M_DK), F32),
                   jax.ShapeDtypeStruct((batch, M_HEADS, M_DK), F32),
                   jax.ShapeDtypeStruct((batch, M_HEADS, LANES), F32)),
        grid=(batch, nc),
        in_specs=[rowblk(COL["mq"] // 1024), rowblk(COL["mk"] // 1024), rowblk(COL["mv"] // 1024),
                  rowblk(COL["mo"] // 1024),
                  pl.BlockSpec((L, LANES), lambda b, c: (b * nc + c, 0)),
                  vec_r, vec_r, st4, st3, st3],
        out_specs=(pl.BlockSpec((L, 1024), lambda b, c: (b * nc + c, 0)), st4, st3, st3),
        compiler_params=_cparams(("arbitrary", "arbitrary")),
        name="mlstm",
    )(p_main, p_main, p_main, p_main, p_small,
      b_ig.reshape(1, M_HEADS), b_fg.reshape(1, M_HEADS), c0, n0, m0)


def _col_reduce(x, op, final):
    r = x.reshape(x.shape[0] // SUBLANES, SUBLANES, x.shape[1])
    while r.shape[0] > 1:
        half = r.shape[0] // 2
        s = op(r[:half], r[half:2 * half])
        r = s if r.shape[0] % 2 == 0 else jnp.concatenate([s, r[2 * half:]], axis=0)
    return final(r[0], axis=0, keepdims=True)


def _colsum(x):
    return _col_reduce(x, jnp.add, jnp.sum)


def _colmax(x):
    return _col_reduce(x, jnp.maximum, jnp.max)


def _count0(mask):
    return _colsum(jnp.where(mask, 1.0, 0.0))


def _dsa_body(q_ref, qi_ref, sm_ref, out_ref, kcat_s, vt_s, ki_s, sel_s, *, sc, s_real, nb, tq, base_pos, topk, qb):
    hpg = A_HEADS // A_KV
    lane = lax.broadcasted_iota(jnp.int32, (1, QLANES), 1)
    qpos = base_pos + qb * tq + lane % tq
    chunk_end = (qpos // CHUNK + 1) * CHUNK
    sidx = lax.broadcasted_iota(jnp.int32, (sc, 1), 0)
    admissible = sidx < chunk_end
    row_b = lax.broadcasted_iota(jnp.int32, (QLANES, 1), 0) // tq

    def slot_rows(x, bb):
        return x if nb == 1 else jnp.where(row_b == bb, x, 0.0)

    wit = sm_ref[...].T[SCOL["iw"]:SCOL["iw"] + I_HEADS, :]
    score = None
    for h in range(I_HEADS):
        qh = qi_ref[:, h * I_DH:(h + 1) * I_DH].astype(F32) * (I_DH ** -0.5)
        lg = None
        for bb in range(nb):
            part = _dot_nt(ki_s[bb, 0:sc, :], slot_rows(qh, bb).astype(BF16))
            lg = part if lg is None else lg + part
        term = jnp.maximum(lg, 0.0) * wit[h:h + 1, :]
        score = term if score is None else score + term
    score = score * (I_HEADS ** -0.5) + 0.0
    score = jnp.where(admissible, score, NEG)
    if sc > s_real:
        score = jnp.where(sidx < s_real, score, -jnp.inf)

    kf = float(topk)

    def key_to_float(key):
        return pltpu.bitcast(jnp.where(key < 0, key ^ jnp.int32(0x7FFFFFFF), key), F32)

    def thr_body(i, ans):
        cand = ans + lax.shift_left(jnp.int32(1), jnp.int32(31) - i)
        ok = _count0(score >= key_to_float(cand)) >= kf
        return jnp.where(ok, cand, ans)

    thr_key = lax.fori_loop(0, 32, thr_body, jnp.full((1, QLANES), jnp.iinfo(jnp.int32).min, jnp.int32))
    thr = key_to_float(thr_key)
    gt_mask = score > thr
    eq_mask = score == thr
    need = kf - _count0(gt_mask)
    n_eq = _count0(eq_mask)
    sel_s[0:sc, :] = jnp.where(admissible, jnp.where(score >= thr, 1.0, 0.0), 0.0)

    @pl.when(jnp.max(n_eq - need) > 0.0)
    def _():
        def idx_body(i, j):
            cand = j + lax.shift_left(jnp.int32(1), jnp.int32(12) - i)
            ok = _count0(eq_mask & (sidx < cand)) < need
            return jnp.where(ok, cand, j)

        jstar = lax.fori_loop(0, 13, idx_body, jnp.zeros((1, QLANES), jnp.int32))
        sel = gt_mask | (eq_mask & (sidx <= jstar))
        sel_s[0:sc, :] = jnp.where(admissible, jnp.where(sel, 1.0, 0.0), 0.0)

    mask = sel_s[0:sc, :] > 0.5

    lane4_b = (lax.broadcasted_iota(jnp.int32, (1, hpg * QLANES), 1) % QLANES) // tq
    row4_b = (lax.broadcasted_iota(jnp.int32, (hpg * QLANES, 1), 0) % QLANES) // tq
    for j in range(A_KV):
        q4 = jnp.concatenate([q_ref[:, (j * hpg + g) * A_DH:(j * hpg + g + 1) * A_DH] for g in range(hpg)], axis=0)
        if nb > 1:
            q4f = q4.astype(F32)
            q4 = jnp.concatenate([jnp.where(row4_b == bb, q4f, 0.0).astype(BF16) for bb in range(nb)], axis=1)
        st = _dot_nt(kcat_s[j, 0:sc, :], q4)
        ps, ls = [], []
        for g in range(hpg):
            s = jnp.where(mask, st[:, g * QLANES:(g + 1) * QLANES], NEG)
            mx = _colmax(s)
            p = jnp.exp2(s - mx)
            ls.append(_colsum(p))
            ps.append(p.astype(BF16))
        p4 = jnp.concatenate(ps, axis=1)
        l4 = jnp.concatenate(ls, axis=1)
        o4 = None
        for bb in range(nb):
            ob = _dot(vt_s[j, bb, :, 0:sc], p4)
            o4 = ob if o4 is None else jnp.where(lane4_b == bb, ob, o4)
        o4 = o4 / l4
        for g in range(hpg):
            h = j * hpg + g
            out_ref[:, h * A_DH:(h + 1) * A_DH] = o4[:, g * QLANES:(g + 1) * QLANES].T.astype(out_ref.dtype)


def _dsa_kernel(*refs, tq, nb, s_real, s_pad, s_past, seq, base_pos, topk, classes, steps_per_class):
    if s_past:
        (q_ref, qi_ref, sm_ref, kn_ref, vn_ref, kin_ref, pk_ref, pv_ref, pki_ref,
         out_ref, kcat_s, vt_s, ki_s, sel_s, vtmp_s) = refs
    else:
        (q_ref, qi_ref, sm_ref, kn_ref, vn_ref, kin_ref, out_ref, kcat_s, vt_s, ki_s, sel_s, vtmp_s) = refs
    qb = pl.program_id(1)

    @pl.when(qb == 0)
    def _():
        for bb in range(nb):
            rows = slice(bb * seq, (bb + 1) * seq)
            if s_past:
                ki_s[bb, 0:s_past, :] = pki_ref[bb].astype(BF16)
            ki_s[bb, s_past:s_real, :] = kin_ref[rows, :].astype(BF16)
            if s_pad > s_real:
                ki_s[bb, s_real:s_pad, :] = jnp.zeros((s_pad - s_real, I_DH), BF16)
            for j in range(A_KV):
                cols = slice(j * A_DH, (j + 1) * A_DH)
                lanes = slice(bb * A_DH, (bb + 1) * A_DH)
                if s_past:
                    kcat_s[j, 0:s_past, lanes] = pk_ref[bb, pl.ds(j, s_past, stride=A_KV), :].astype(BF16)
                    vtmp_s[0:s_past, :] = pv_ref[bb, pl.ds(j, s_past, stride=A_KV), :]
                kcat_s[j, s_past:s_real, lanes] = kn_ref[rows, cols].astype(BF16)
                vtmp_s[s_past:s_real, :] = vn_ref[rows, cols].astype(F32)
                if s_pad > s_real:
                    kcat_s[j, s_real:s_pad, lanes] = jnp.zeros((s_pad - s_real, A_DH), BF16)
                    vtmp_s[s_real:s_pad, :] = jnp.zeros((s_pad - s_real, A_DH), F32)
                vt_s[j, bb] = vtmp_s[...].T.astype(BF16)

    body = functools.partial(_dsa_body, q_ref, qi_ref, sm_ref, out_ref, kcat_s, vt_s, ki_s, sel_s,
                             s_real=s_real, nb=nb, tq=tq, base_pos=base_pos, topk=topk, qb=qb)
    if len(classes) == 1:
        body(sc=classes[0])
    else:
        for c, sc in enumerate(classes):
            pl.when(qb // steps_per_class == c)(functools.partial(body, sc=sc))


def dsa(q_r, qi_r, p_small, k_new, v_src, v_colblk, ki_new, past, batch, seq, base_pos):
    m = batch * seq
    tq = min(seq, QLANES)
    nb = QLANES // tq
    nq = seq // tq
    s_past = 0 if past is None else past[2].shape[1]
    s_real = s_past + seq
    s_pad = -(-s_real // LANES) * LANES
    topk = min(TOPK_MAX, s_real // 4)
    assert topk <= s_real and s_pad < 8192 and batch % nb == 0
    n_cls = 8 if (past is None and nb == 1 and nq % 8 == 0 and s_pad % (8 * LANES) == 0
                  and s_pad // 8 >= topk) else 1
    classes = tuple(s_pad * (c + 1) // n_cls for c in range(n_cls))
    assert n_cls == 1 or base_pos == 0
    qrow = lambda w: pl.BlockSpec((QLANES, w), lambda b, i: (b * nq + i, 0))
    in_specs = [qrow(1024), qrow(512), qrow(LANES),
                pl.BlockSpec((nb * seq, 256), lambda b, i: (b, 0)),
                pl.BlockSpec((nb * seq, 256), lambda b, i: (b, v_colblk)),
                pl.BlockSpec((nb * seq, I_DH), lambda b, i: (b, 0))]
    args = [q_r, qi_r, p_small, k_new, v_src, ki_new]
    if past is not None:
        in_specs += [pl.BlockSpec((nb, s_past * A_KV, A_DH), lambda b, i: (b, 0, 0)),
                     pl.BlockSpec((nb, s_past * A_KV, A_DH), lambda b, i: (b, 0, 0)),
                     pl.BlockSpec((nb, s_past, I_DH), lambda b, i: (b, 0, 0))]
        args += list(past)
    kern = functools.partial(_dsa_kernel, tq=tq, nb=nb, s_real=s_real, s_pad=s_pad, s_past=s_past, seq=seq,
                             base_pos=base_pos, topk=topk, classes=classes, steps_per_class=nq // n_cls)
    return pl.pallas_call(
        kern,
        out_shape=jax.ShapeDtypeStruct((m, A_HEADS * A_DH), BF16),
        grid=(batch // nb, nq),
        in_specs=in_specs,
        out_specs=pl.BlockSpec((QLANES, 1024), lambda b, i: (b * nq + i, 0)),
        scratch_shapes=[pltpu.VMEM((A_KV, s_pad, nb * A_DH), BF16),
                        pltpu.VMEM((A_KV, nb, A_DH, s_pad), BF16),
                        pltpu.VMEM((nb, s_pad, I_DH), BF16),
                        pltpu.VMEM((s_pad, QLANES), F32),
                        pltpu.VMEM((s_pad, A_DH), F32)],
        compiler_params=_cparams(("arbitrary", "arbitrary")),
        name="dsa",
    )(*args)


def _mem_attn_kernel(q_ref, k_ref, v_ref, o_ref, *, rows_by_head):
    for h in range(C_HEADS):
        sl = slice(h * C_DH, (h + 1) * C_DH)
        if rows_by_head:
            parts = C_DH // LANES
            step = C_HEADS * parts
            kh = jnp.concatenate([k_ref[pl.ds(c * C_HEADS + h, N_MEM, stride=step), :] for c in range(parts)], axis=1)
            vh = jnp.concatenate([v_ref[pl.ds(c * C_HEADS + h, N_MEM, stride=step), :] for c in range(parts)], axis=1)
        else:
            kh, vh = k_ref[:, sl], v_ref[:, sl]
        s = _dot_nt(q_ref[:, sl].astype(BF16), kh.astype(BF16)) * (C_DH ** -0.5)
        mx = jnp.max(s, axis=-1, keepdims=True)
        p = jnp.exp(s - mx)
        l = jnp.sum(p, axis=-1, keepdims=True)
        o_ref[:, sl] = (_dot(p.astype(BF16), vh.astype(BF16)) / l).astype(o_ref.dtype)


def _mem_rows(cache):
    b = cache.shape[0]
    parts = C_DH // LANES
    x = cache.reshape(b, N_MEM, C_HEADS, parts, LANES).transpose(0, 1, 3, 2, 4)
    return x.reshape(b * N_MEM * parts * C_HEADS, LANES)


def mem_attn(p_main, mk, mk_colblk, mv, mv_colblk, batch, seq, tq):
    m = batch * seq
    nq = seq // tq
    w = C_HEADS * C_DH
    rows_by_head = mk_colblk is None
    if rows_by_head:
        kv_specs = [pl.BlockSpec((N_MEM * w // LANES, LANES), lambda b, i: (b, 0))] * 2
    else:
        kv_specs = [pl.BlockSpec((N_MEM, w), lambda b, i: (b, mk_colblk)),
                    pl.BlockSpec((N_MEM, w), lambda b, i: (b, mv_colblk))]
    return pl.pallas_call(
        functools.partial(_mem_attn_kernel, rows_by_head=rows_by_head),
        out_shape=jax.ShapeDtypeStruct((m, w), BF16),
        grid=(batch, nq),
        in_specs=[pl.BlockSpec((tq, w), lambda b, i: (b * nq + i, COL["cq"] // w))] + kv_specs,
        out_specs=pl.BlockSpec((tq, w), lambda b, i: (b * nq + i, 0)),
        compiler_params=_cparams(("arbitrary", "arbitrary")),
        name="mem_attn",
    )(p_main, mk, mv)


def _merge_kernel(a_ref, b_ref, c_ref, ga_ref, gb_ref, gc_ref, w_ref, o_ref):
    acc = None
    for br_ref, g_ref, n in ((a_ref, ga_ref, 0), (b_ref, gb_ref, 1), (c_ref, gc_ref, 2)):
        proj = _dot(br_ref[...], w_ref[n])
        term = proj / (1.0 + jnp.exp(-g_ref[...].astype(F32)))
        acc = term if acc is None else acc + term
    o_ref[...] = acc.astype(o_ref.dtype)


def merge(br_a, br_b, br_c, p_main, w_branch, tm, tn):
    m = br_a.shape[0]
    nj = D_MODEL // tn
    brs = pl.BlockSpec((tm, BRANCH_W), lambda i, j: (i, 0))
    gate = lambda n: pl.BlockSpec((tm, tn), lambda i, j: (i, (COL["gt"] + n * D_MODEL) // tn + j))
    return pl.pallas_call(
        _merge_kernel,
        out_shape=jax.ShapeDtypeStruct((m, D_MODEL), BF16),
        grid=(m // tm, nj),
        in_specs=[brs, brs, brs, gate(0), gate(1), gate(2),
                  pl.BlockSpec((N_BRANCH, BRANCH_W, tn), lambda i, j: (0, 0, j))],
        out_specs=pl.BlockSpec((tm, tn), lambda i, j: (i, j)),
        compiler_params=_cparams(("arbitrary", "arbitrary")),
        name="merge",
    )(br_a, br_b, br_c, p_main, p_main, p_main, w_branch)


def _ffn_kernel(h_ref, g_ref, wg_ref, wv_ref, cwg_ref, cwv_ref, cbg_ref, cbv_ref, pg_ref, pv_ref, wd_ref, gf_ref,
                y_ref, tg_ref, tv_ref, xn_ref, ug_s, uv_s, cg_s, cv_s, *, tm, rb, seg, blocks_per_seq):
    i = pl.program_id(0)
    j = pl.program_id(1)
    hdr = SUBLANES
    sub = min(seg, rb)

    @pl.when(j == 0)
    def _():
        x = h_ref[...]
        ms = jnp.mean(x * x, axis=-1, keepdims=True)
        xn_ref[...] = (x * lax.rsqrt(ms + EPS) * g_ref[...]).astype(BF16)
        y_ref[...] = jnp.zeros_like(y_ref)

    if blocks_per_seq > 1:
        @pl.when((i % blocks_per_seq) == 0)
        def _():
            cg_s[j] = pg_ref[0]
            cv_s[j] = pv_ref[0]

    def conv(u, prev, cw_ref, cb_ref, u_s):
        u_s[hdr - 2:hdr, :] = prev
        u_s[hdr:hdr + sub, :] = u
        return (cb_ref[...] + u_s[hdr - 2:hdr - 2 + sub, :] * cw_ref[0:1, :]
                + u_s[hdr - 1:hdr - 1 + sub, :] * cw_ref[1:2, :] + u * cw_ref[2:3, :])

    tail_g = tail_v = None
    for r in range(tm // rb):
        xn = xn_ref[r * rb:(r + 1) * rb, :]
        ug = _dot(xn, wg_ref[...])
        uv = _dot(xn, wv_ref[...])
        acts = []
        for s in range(rb // sub):
            row0 = r * rb + s * sub
            sq = row0 // seg
            rows = slice(s * sub, (s + 1) * sub)
            ugs, uvs = ug[rows, :], uv[rows, :]
            if row0 % seg == 0:
                prev_g = cg_s[j] if blocks_per_seq > 1 else pg_ref[sq]
                prev_v = cv_s[j] if blocks_per_seq > 1 else pv_ref[sq]
            else:
                prev_g, prev_v = tail_g, tail_v
            cg = conv(ugs, prev_g, cwg_ref, cbg_ref, ug_s.at[r])
            cv = conv(uvs, prev_v, cwv_ref, cbv_ref, uv_s.at[r])
            acts.append((cg / (1.0 + jnp.exp(-cg)) * cv).astype(BF16))
            tail_g, tail_v = ugs[sub - 2:sub, :], uvs[sub - 2:sub, :]
            if (row0 + sub) % seg == 0:
                tg_ref[sq] = tail_g
                tv_ref[sq] = tail_v
                if blocks_per_seq > 1:
                    cg_s[j] = tail_g
                    cv_s[j] = tail_v
        act = acts[0] if len(acts) == 1 else jnp.concatenate(acts, axis=0)
        cw = 512
        for c in range(D_MODEL // cw):
            cols = slice(c * cw, (c + 1) * cw)
            y_ref[r * rb:(r + 1) * rb, cols] += _dot(act, wd_ref[:, cols])

    @pl.when(j == pl.num_programs(1) - 1)
    def _():
        z = h_ref[...] + y_ref[...]
        ms = jnp.mean(z * z, axis=-1, keepdims=True)
        y_ref[...] = z * lax.rsqrt(ms + EPS) * gf_ref[...]


def ffn(h, g_ffn, w_up, conv_w, conv_b, conv_prev, w_down, g_final, seq, tm, tn):
    m = h.shape[0]
    seg = min(tm, seq)
    nseg = tm // seg
    bps = seq // seg
    blocks_per_seq = bps if nseg == 1 else 1
    nj = D_FF // tn
    prev_spec = lambda off: pl.BlockSpec((nseg, CONV_W - 1, tn),
                                         lambda i, j: ((i // blocks_per_seq) if nseg == 1 else i, 0, off + j))
    tail_spec = pl.BlockSpec((nseg, CONV_W - 1, tn), lambda i, j: (i, 0, j))
    rb = min(tm, 512)
    sub = min(seg, rb)
    kern = functools.partial(_ffn_kernel, tm=tm, rb=rb, seg=seg, blocks_per_seq=blocks_per_seq)
    return pl.pallas_call(
        kern,
        out_shape=(jax.ShapeDtypeStruct((m, D_MODEL), F32),
                   jax.ShapeDtypeStruct((m // seg, CONV_W - 1, D_FF), F32),
                   jax.ShapeDtypeStruct((m // seg, CONV_W - 1, D_FF), F32)),
        grid=(m // tm, nj),
        in_specs=[pl.BlockSpec((tm, D_MODEL), lambda i, j: (i, 0), pipeline_mode=pl.Buffered(1)),
                  pl.BlockSpec((1, D_MODEL), lambda i, j: (0, 0)),
                  pl.BlockSpec((D_MODEL, tn), lambda i, j: (0, j)),
                  pl.BlockSpec((D_MODEL, tn), lambda i, j: (0, nj + j)),
                  pl.BlockSpec((CONV_W, tn), lambda i, j: (0, j)),
                  pl.BlockSpec((CONV_W, tn), lambda i, j: (0, nj + j)),
                  pl.BlockSpec((1, tn), lambda i, j: (0, j)),
                  pl.BlockSpec((1, tn), lambda i, j: (0, nj + j)),
                  prev_spec(0), prev_spec(nj),
                  pl.BlockSpec((tn, D_MODEL), lambda i, j: (j, 0)),
                  pl.BlockSpec((1, D_MODEL), lambda i, j: (0, 0))],
        out_specs=(pl.BlockSpec((tm, D_MODEL), lambda i, j: (i, 0), pipeline_mode=pl.Buffered(1)),
                   tail_spec, tail_spec),
        scratch_shapes=[pltpu.VMEM((tm, D_MODEL), BF16),
                        pltpu.VMEM((tm // rb, SUBLANES + sub, tn), F32),
                        pltpu.VMEM((tm // rb, SUBLANES + sub, tn), F32),
                        pltpu.VMEM((nj, CONV_W - 1, tn), F32), pltpu.VMEM((nj, CONV_W - 1, tn), F32)],
        compiler_params=pltpu.CompilerParams(dimension_semantics=("arbitrary", "arbitrary"),
                                             vmem_limit_bytes=FFN_VMEM_LIMIT),
        name="ffn",
    )(h, g_ffn.reshape(1, D_MODEL), w_up, w_up, conv_w, conv_w,
      conv_b.reshape(1, 2 * D_FF), conv_b.reshape(1, 2 * D_FF), conv_prev, conv_prev,
      w_down, g_final.reshape(1, D_MODEL))


def _group(x, pos, past, mstate, mem_k, mem_kblk, mem_v, mem_vblk, conv_prev, wts, base_pos):
    (g_mix, w_main, w_small, b_ig, b_fg, w_branch, w_out, g_ffn, w_up, conv_w, conv_b, w_down, g_final) = wts
    batch, seq, _ = x.shape
    m = batch * seq
    x2 = x.reshape(m, D_MODEL)

    p_main = norm_matmul(x2, g_mix, w_main, 1024, 1024, "in_proj", out_dtype=BF16)
    p_small = norm_matmul(x2, g_mix, w_small, 1024, LANES, "in_proj_small")

    q_r, qi_r, k_new, ki_new = rope(p_main, p_small, pos, seq, 512)
    v_new = p_main[:, COL["av"]:COL["av"] + A_KV * A_DH].astype(F32)

    c0, n0, m0 = mstate
    br_a, c_new, n_new, m_new = mlstm(p_main, p_small, b_ig, b_fg, c0, n0,
                                      jnp.broadcast_to(m0[:, :, None], (batch, M_HEADS, LANES)), batch, seq)
    br_b = dsa(q_r, qi_r, p_small, k_new, p_main, COL["av"] // 256, ki_new, past, batch, seq, base_pos)
    br_c = mem_attn(p_main, mem_k, mem_kblk, mem_v, mem_vblk, batch, seq, min(seq, 512))

    merged = merge(br_a, br_b, br_c, p_main, w_branch, 1024, 512)
    h = matmul_res(merged, w_out, x2, 1024, 1024, "out_proj")

    y, tail_g, tail_v = ffn(h, g_ffn, w_up, conv_w, conv_b, conv_prev, w_down, g_final, seq, 1024, 512)

    nblk = tail_g.shape[0] // batch
    conv_new = jnp.concatenate([tail_g.reshape(batch, nblk, CONV_W - 1, D_FF)[:, -1],
                                tail_v.reshape(batch, nblk, CONV_W - 1, D_FF)[:, -1]], axis=-1)
    outs = (y.reshape(batch, seq, D_MODEL),
            k_new.reshape(batch, seq, A_KV, A_DH), v_new.reshape(batch, seq, A_KV, A_DH),
            ki_new.reshape(batch, seq, I_DH), c_new, n_new, m_new[:, :, 0], conv_new)
    return outs


def kernel(x_prompt, x_sample, mem_prompt, cache_dsa_k, cache_dsa_v, cache_dsa_kidx, state_mlstm_C, state_mlstm_n, state_mlstm_m, state_ffn_conv, cache_mem_k, cache_mem_v, g_mix, w_in, b_igate, b_fgate, w_branch, w_out, g_mem, w_mem_kv, g_ffn, w_up, conv_w, conv_b, w_down, g_final):
    depth = w_in.shape[0]
    assert depth == 1
    B, T = x_prompt.shape[:2]
    DB, DT = x_sample.shape[:2]
    P = cache_dsa_k.shape[2]
    l = 0

    w = w_in[l]
    segs = {}
    off = 0
    for name, width in zip(ORIG_NAMES, ORIG_SPLITS):
        segs[name] = w[:, off:off + width]
        off += width
    w_main = jnp.concatenate([segs[n] for n in MAIN_ORDER], axis=1).astype(BF16)
    small_w = sum(ORIG_SPLITS[ORIG_NAMES.index(n)] for n in SMALL_ORDER)
    w_small = jnp.concatenate([segs[n] for n in SMALL_ORDER] + [jnp.zeros((D_MODEL, LANES - small_w), F32)],
                              axis=1).astype(BF16)
    wts = (g_mix[l], w_main, w_small, b_igate[l], b_fgate[l], w_branch[l].astype(BF16), w_out[l].astype(BF16),
           g_ffn[l], w_up[l].astype(BF16), conv_w[l], conv_b[l], w_down[l].astype(BF16), g_final)

    mkv = norm_matmul(mem_prompt.reshape(B * N_MEM, D_MODEL), g_mem[l], w_mem_kv[l].astype(BF16), 512, 1024, "mem_kv")
    zero_state = (jnp.zeros((B, M_HEADS, M_DV, M_DK), F32), jnp.zeros((B, M_HEADS, M_DK), F32),
                  jnp.zeros((B, M_HEADS), F32))
    zero_conv = jnp.zeros((B, CONV_W - 1, 2 * D_FF), F32)
    (y_p, k_p, v_p, ki_p, c_p, n_p, m_p, conv_p) = _group(
        x_prompt, jnp.arange(T, dtype=jnp.int32), None, zero_state, mkv, 0, mkv, 1, zero_conv, wts, 0)
    hw = C_HEADS * C_DH
    mk_p = mkv[:, :hw].reshape(B, N_MEM, C_HEADS, C_DH)
    mv_p = mkv[:, hw:].reshape(B, N_MEM, C_HEADS, C_DH)

    past = (cache_dsa_k[l].reshape(DB, P * A_KV, A_DH), cache_dsa_v[l].reshape(DB, P * A_KV, A_DH),
            cache_dsa_kidx[l])
    mstate = (state_mlstm_C[l], state_mlstm_n[l], state_mlstm_m[l])
    (y_s, k_s, v_s, ki_s, c_s, n_s, m_s, conv_s) = _group(
        x_sample, P + jnp.arange(DT, dtype=jnp.int32), past, mstate,
        _mem_rows(cache_mem_k[l]), None, _mem_rows(cache_mem_v[l]), None,
        state_ffn_conv[l], wts, P)

    st = lambda a: a[None]
    return (y_p, y_s,
            st(k_p), st(v_p), st(ki_p), st(c_p), st(n_p), st(m_p), st(conv_p), st(mk_p), st(mv_p),
            st(k_s), st(v_s), st(ki_s), st(c_s), st(n_s), st(m_s), st(conv_s))
```

```python
import functools
import math

import jax
import jax.numpy as jnp
from jax import lax
from jax.experimental import pallas as pl
from jax.experimental.pallas import tpu as pltpu

F32 = jnp.float32
BF16 = jnp.bfloat16

D_MODEL = 2048
CHUNK = 64
MLSTM_CHUNK = 256
M_HEADS, M_DK, M_DV = 8, 128, 128
A_HEADS, A_KV, A_DH = 8, 2, 128
I_HEADS, I_DH = 8, 64
TOPK_MAX = 256
N_MEM, C_HEADS, C_DH = 256, 4, 256
N_BRANCH, BRANCH_W = 3, 1024
D_FF = 5632
CONV_W = 3
ROPE_THETA = 500000.0
ROPE_FRAC = 4
EPS = 1e-6
NEG = -1e30

LANES = 128
SUBLANES = 8
VMEM_LIMIT = 48 * 1024 * 1024
FFN_VMEM_LIMIT = 56 * 1024 * 1024

QLANES = LANES
Q_SCALE = A_DH ** -0.5 * math.log2(math.e)

ORIG_SPLITS = (1024, 1024, 1024, 1024, 8, 8, 1024, 256, 256, 512, 64, 8, 1024, 6144)
ORIG_NAMES = ("mq", "mk", "mv", "mo", "mi", "mf", "aq", "ak", "av", "iq", "ik", "iw", "cq", "gt")
MAIN_ORDER = ("mq", "mk", "mv", "mo", "aq", "cq", "gt", "ak", "av", "iq")
SMALL_ORDER = ("ik", "mi", "mf", "iw")
MAIN_W = 13312
COL = {}
_off = 0
for _n in MAIN_ORDER:
    COL[_n] = _off
    _off += ORIG_SPLITS[ORIG_NAMES.index(_n)]
assert _off == MAIN_W
SCOL = {"ik": 0, "mi": 64, "mf": 72, "iw": 80}


def _cparams(sem):
    return pltpu.CompilerParams(dimension_semantics=sem, vmem_limit_bytes=VMEM_LIMIT)


def _dot(a, b):
    return jnp.dot(a, b, preferred_element_type=F32)


def _dot_nt(a, b):
    return lax.dot_general(a, b, (((1,), (1,)), ((), ())), preferred_element_type=F32)


def _dot_tn(a, b):
    return lax.dot_general(a, b, (((0,), (0,)), ((), ())), preferred_element_type=F32)


def _norm_matmul_kernel(x_ref, g_ref, w_ref, o_ref, xn_ref):
    @pl.when(pl.program_id(1) == 0)
    def _():
        x = x_ref[...]
        ms = jnp.mean(x * x, axis=-1, keepdims=True)
        xn_ref[...] = (x * lax.rsqrt(ms + EPS) * g_ref[...]).astype(BF16)

    o_ref[...] = _dot(xn_ref[...], w_ref[...]).astype(o_ref.dtype)


def norm_matmul(x, g, w, tm, tn, name, out_dtype=F32):
    m, k = x.shape
    n = w.shape[1]
    return pl.pallas_call(
        _norm_matmul_kernel,
        out_shape=jax.ShapeDtypeStruct((m, n), out_dtype),
        grid=(m // tm, n // tn),
        in_specs=[pl.BlockSpec((tm, k), lambda i, j: (i, 0)),
                  pl.BlockSpec((1, k), lambda i, j: (0, 0)),
                  pl.BlockSpec((k, tn), lambda i, j: (0, j))],
        out_specs=pl.BlockSpec((tm, tn), lambda i, j: (i, j)),
        scratch_shapes=[pltpu.VMEM((tm, k), BF16)],
        compiler_params=_cparams(("arbitrary", "arbitrary")),
        name=name,
    )(x, g.reshape(1, k), w)


def _matmul_res_kernel(a_ref, w_ref, r_ref, o_ref):
    o_ref[...] = _dot(a_ref[...], w_ref[...]) + r_ref[...]


def matmul_res(a, w, res, tm, tn, name):
    m, k = a.shape
    n = w.shape[1]
    return pl.pallas_call(
        _matmul_res_kernel,
        out_shape=jax.ShapeDtypeStruct((m, n), F32),
        grid=(m // tm, n // tn),
        in_specs=[pl.BlockSpec((tm, k), lambda i, j: (i, 0)),
                  pl.BlockSpec((k, tn), lambda i, j: (0, j)),
                  pl.BlockSpec((tm, tn), lambda i, j: (i, j))],
        out_specs=pl.BlockSpec((tm, tn), lambda i, j: (i, j)),
        compiler_params=_cparams(("arbitrary", "arbitrary")),
        name=name,
    )(a, w, res)


def _rope_tables(pos, dh):
    r = dh // ROPE_FRAC
    half = r // 2
    inv = ROPE_THETA ** (-jnp.arange(half, dtype=F32) * 2.0 / r)
    ang = pos.astype(F32)[:, None] * inv[None, :]
    cos, sin = jnp.cos(ang), jnp.sin(ang)
    t = pos.shape[0]
    ones = jnp.ones((t, dh - r), F32)
    zeros_h = jnp.zeros((t, half), F32)
    zeros_rest = jnp.zeros((t, dh - r), F32)
    c = jnp.concatenate([cos, cos, ones], axis=1)
    sa = jnp.concatenate([zeros_h, sin, zeros_rest], axis=1)
    sb = jnp.concatenate([-sin, zeros_h, zeros_rest], axis=1)
    rep = LANES // dh
    return tuple(jnp.tile(a, (1, rep)) for a in (c, sa, sb))


def _rope_apply(x, c, sa, sb, half):
    return x * c + pltpu.roll(x, half, 1) * sa + pltpu.roll(x, LANES - half, 1) * sb


def _rope_kernel(aq_ref, ak_ref, iq_ref, sm_ref, c128_ref, sa128_ref, sb128_ref, c64_ref, sa64_ref, sb64_ref,
                 q_ref, qi_ref, k_ref, ki_ref):
    c1, a1, b1 = c128_ref[...], sa128_ref[...], sb128_ref[...]
    c2, a2, b2 = c64_ref[...], sa64_ref[...], sb64_ref[...]
    h1 = A_DH // ROPE_FRAC // 2
    h2 = I_DH // ROPE_FRAC // 2
    for h in range(A_HEADS):
        sl = slice(h * LANES, (h + 1) * LANES)
        q_ref[:, sl] = (_rope_apply(aq_ref[:, sl].astype(F32), c1, a1, b1, h1) * Q_SCALE).astype(q_ref.dtype)
    for h in range(A_KV):
        sl = slice(h * LANES, (h + 1) * LANES)
        k_ref[:, sl] = _rope_apply(ak_ref[:, sl].astype(F32), c1, a1, b1, h1)
    for h in range(I_HEADS * I_DH // LANES):
        sl = slice(h * LANES, (h + 1) * LANES)
        qi_ref[:, sl] = _rope_apply(iq_ref[:, sl].astype(F32), c2, a2, b2, h2).astype(qi_ref.dtype)
    ki = _rope_apply(sm_ref[...], c2, a2, b2, h2)
    ki_ref[...] = ki[:, :I_DH]


def rope(p_main, p_small, pos, seq, tr):
    m = p_main.shape[0]
    t128 = _rope_tables(pos, A_DH)
    t64 = _rope_tables(pos, I_DH)
    if seq < tr:
        t128 = tuple(jnp.tile(a, (tr // seq, 1)) for a in t128)
        t64 = tuple(jnp.tile(a, (tr // seq, 1)) for a in t64)
    ntab = max(seq, tr) // tr
    tab_spec = pl.BlockSpec((tr, LANES), lambda i: (i % ntab, 0))
    return pl.pallas_call(
        _rope_kernel,
        out_shape=(jax.ShapeDtypeStruct((m, A_HEADS * A_DH), BF16),
                   jax.ShapeDtypeStruct((m, I_HEADS * I_DH), BF16),
                   jax.ShapeDtypeStruct((m, A_KV * A_DH), F32),
                   jax.ShapeDtypeStruct((m, I_DH), F32)),
        grid=(m // tr,),
        in_specs=[pl.BlockSpec((tr, 1024), lambda i: (i, COL["aq"] // 1024)),
                  pl.BlockSpec((tr, 256), lambda i: (i, COL["ak"] // 256)),
                  pl.BlockSpec((tr, 512), lambda i: (i, COL["iq"] // 512)),
                  pl.BlockSpec((tr, LANES), lambda i: (i, 0)),
                  tab_spec, tab_spec, tab_spec, tab_spec, tab_spec, tab_spec],
        out_specs=(pl.BlockSpec((tr, 1024), lambda i: (i, 0)),
                   pl.BlockSpec((tr, 512), lambda i: (i, 0)),
                   pl.BlockSpec((tr, 256), lambda i: (i, 0)),
                   pl.BlockSpec((tr, I_DH), lambda i: (i, 0))),
        compiler_params=_cparams(("arbitrary",)),
        name="rope",
    )(p_main, p_main, p_main, p_small, *t128, *t64)


def _split3(x):
    hi = x.astype(BF16)
    r1 = x - hi.astype(F32)
    mid = r1.astype(BF16)
    lo = (r1 - mid.astype(F32)).astype(BF16)
    return hi, mid, lo


def _log_sigmoid(x):
    return jnp.minimum(x, 0.0) - jnp.log1p(jnp.exp(-jnp.abs(x)))


def _mlstm_kernel(q_ref, k_ref, v_ref, o_ref, sm_ref, big_r_ref, bfg_r_ref,
                  c0_ref, n0_ref, m0_ref, h_ref, c_ref, n_ref, m_ref, *, L):

    @pl.when(pl.program_id(1) == 0)
    def _():
        c_ref[...] = c0_ref[...]
        n_ref[...] = n0_ref[...]
        m_ref[...] = m0_ref[...]

    g = sm_ref[...]
    ig_c = g[:, SCOL["mi"]:SCOL["mi"] + M_HEADS] + big_r_ref[...]
    lf_c = _log_sigmoid(g[:, SCOL["mf"]:SCOL["mf"] + M_HEADS] + bfg_r_ref[...])

    row = lax.broadcasted_iota(jnp.int32, (L, L), 0)
    col = lax.broadcasted_iota(jnp.int32, (L, L), 1)
    causal = col <= row
    tril = jnp.where(causal, 1.0, 0.0).astype(BF16)
    triu = jnp.where(row <= col, 1.0, 0.0).astype(BF16)
    eye = jnp.where(row == col, 1.0, 0.0).astype(BF16)
    lf_parts = _split3(lf_c)
    b_c = sum(_dot(tril, part) for part in lf_parts)
    b_r = sum(_dot_tn(part, triu) for part in lf_parts)
    ig_r = sum(_dot_tn(part, eye) for part in _split3(ig_c))

    scale = M_DK ** -0.5
    for h in range(M_HEADS):
        sl = slice(h * LANES, (h + 1) * LANES)
        bc = b_c[:, h:h + 1]
        br = b_r[h:h + 1, :]
        igr = ig_r[h:h + 1, :]
        igc = ig_c[:, h:h + 1]
        m_prev = m_ref[0, h:h + 1, 0:1]
        c_prev = c_ref[0, h]
        n_prev = n_ref[0, h:h + 1, :]

        log_d = jnp.where(causal, bc - br + igr, -jnp.inf)
        m_inter = bc + m_prev
        m_t = jnp.maximum(m_inter, jnp.max(log_d, axis=-1, keepdims=True))
        d = jnp.exp(log_d - m_t)
        inter = jnp.exp(m_inter - m_t)

        qh = q_ref[:, sl].astype(F32)
        kh = k_ref[:, sl].astype(F32) * scale
        vh = v_ref[:, sl].astype(F32)
        qb = qh.astype(BF16)
        kb = kh.astype(BF16)
        s = _dot_nt(qb, kb) * d
        num = _dot(s.astype(BF16), vh.astype(BF16)) + _dot_nt(qb, c_prev.astype(BF16)) * inter
        den = jnp.sum(s, axis=-1, keepdims=True) + inter * jnp.sum(qh * n_prev, axis=-1, keepdims=True)
        hout = num / jnp.maximum(jnp.abs(den), jnp.exp(-m_t))
        gate = 1.0 / (1.0 + jnp.exp(-o_ref[:, sl].astype(F32)))
        h_ref[:, sl] = (hout * gate).astype(h_ref.dtype)

        m_new = m_t[L - 1:L, :]
        b_last = bc[L - 1:L, :]
        w_c = jnp.exp(b_last - bc + igc - m_new)
        decay = jnp.exp(b_last + m_prev - m_new)
        c_ref[0, h] = decay * c_prev + _dot_tn((vh * w_c).astype(BF16), kb)
        n_ref[0, h:h + 1, :] = decay * n_prev + jnp.sum(kh * w_c, axis=0, keepdims=True)
        m_ref[0, h:h + 1, :] = jnp.broadcast_to(m_new, (1, LANES))


def mlstm(p_main, p_small, b_ig, b_fg, c0, n0, m0, batch, seq):
    L = min(seq, MLSTM_CHUNK)
    nc = seq // L
    m = batch * seq
    rowblk = lambda cb: pl.BlockSpec((L, 1024), lambda b, c: (b * nc + c, cb))
    st4 = pl.BlockSpec((1, M_HEADS, M_DV, M_DK), lambda b, c: (b, 0, 0, 0))
    st3 = pl.BlockSpec((1, M_HEADS, LANES), lambda b, c: (b, 0, 0))
    vec_r = pl.BlockSpec((1, M_HEADS), lambda b, c: (0, 0))
    return pl.pallas_call(
        functools.partial(_mlstm_kernel, L=L),
        out_shape=(jax.ShapeDtypeStruct((m, M_HEADS * M_DV), BF16),
                   jax.ShapeDtypeStruct((batch, M_HEADS, M_DV, M_DK), F32),
                   jax.ShapeDtypeStruct((batch, M_HEADS, M_DK), F32),
                   jax.ShapeDtypeStruct((batch, M_HEADS, LANES), F32)),
        grid=(batch, nc),
        in_specs=[rowblk(COL["mq"] // 1024), rowblk(COL["mk"] // 1024), rowblk(COL["mv"] // 1024),
                  rowblk(COL["mo"] // 1024),
                  pl.BlockSpec((L, LANES), lambda b, c: (b * nc + c, 0)),
                  vec_r, vec_r, st4, st3, st3],
        out_specs=(pl.BlockSpec((L, 1024), lambda b, c: (b * nc + c, 0)), st4, st3, st3),
        compiler_params=_cparams(("arbitrary", "arbitrary")),
        name="mlstm",
    )(p_main, p_main, p_main, p_main, p_small,
      b_ig.reshape(1, M_HEADS), b_fg.reshape(1, M_HEADS), c0, n0, m0)


def _col_reduce(x, op, final):
    r = x.reshape(x.shape[0] // SUBLANES, SUBLANES, x.shape[1])
    while r.shape[0] > 1:
        half = r.shape[0] // 2
        s = op(r[:half], r[half:2 * half])
        r = s if r.shape[0] % 2 == 0 else jnp.concatenate([s, r[2 * half:]], axis=0)
    return final(r[0], axis=0, keepdims=True)


def _colsum(x):
    return _col_reduce(x, jnp.add, jnp.sum)


def _colmax(x):
    return _col_reduce(x, jnp.maximum, jnp.max)


def _count0(mask):
    return _colsum(jnp.where(mask, 1.0, 0.0))


def _dsa_body(q_ref, qi_ref, sm_ref, out_ref, kcat_s, vt_s, ki_s, sel_s, *, sc, s_real, nb, tq, base_pos, topk, qb):
    hpg = A_HEADS // A_KV
    lane = lax.broadcasted_iota(jnp.int32, (1, QLANES), 1)
    qpos = base_pos + qb * tq + lane % tq
    chunk_end = (qpos // CHUNK + 1) * CHUNK
    sidx = lax.broadcasted_iota(jnp.int32, (sc, 1), 0)
    admissible = sidx < chunk_end
    row_b = lax.broadcasted_iota(jnp.int32, (QLANES, 1), 0) // tq

    def slot_rows(x, bb):
        return x if nb == 1 else jnp.where(row_b == bb, x, 0.0)

    wit = sm_ref[...].T[SCOL["iw"]:SCOL["iw"] + I_HEADS, :]
    score = None
    for h in range(I_HEADS):
        qh = qi_ref[:, h * I_DH:(h + 1) * I_DH].astype(F32) * (I_DH ** -0.5)
        lg = None
        for bb in range(nb):
            part = _dot_nt(ki_s[bb, 0:sc, :], slot_rows(qh, bb).astype(BF16))
            lg = part if lg is None else lg + part
        term = jnp.maximum(lg, 0.0) * wit[h:h + 1, :]
        score = term if score is None else score + term
    score = score * (I_HEADS ** -0.5) + 0.0
    score = jnp.where(admissible, score, NEG)
    if sc > s_real:
        score = jnp.where(sidx < s_real, score, -jnp.inf)

    kf = float(topk)

    def key_to_float(key):
        return pltpu.bitcast(jnp.where(key < 0, key ^ jnp.int32(0x7FFFFFFF), key), F32)

    def thr_body(i, ans):
        cand = ans + lax.shift_left(jnp.int32(1), jnp.int32(31) - i)
        ok = _count0(score >= key_to_float(cand)) >= kf
        return jnp.where(ok, cand, ans)

    thr_key = lax.fori_loop(0, 32, thr_body, jnp.full((1, QLANES), jnp.iinfo(jnp.int32).min, jnp.int32))
    thr = key_to_float(thr_key)
    gt_mask = score > thr
    eq_mask = score == thr
    need = kf - _count0(gt_mask)
    n_eq = _count0(eq_mask)
    sel_s[0:sc, :] = jnp.where(admissible, jnp.where(score >= thr, 1.0, 0.0), 0.0)

    @pl.when(jnp.max(n_eq - need) > 0.0)
    def _():
        def idx_body(i, j):
            cand = j + lax.shift_left(jnp.int32(1), jnp.int32(12) - i)
            ok = _count0(eq_mask & (sidx < cand)) < need
            return jnp.where(ok, cand, j)

        jstar = lax.fori_loop(0, 13, idx_body, jnp.zeros((1, QLANES), jnp.int32))
        sel = gt_mask | (eq_mask & (sidx <= jstar))
        sel_s[0:sc, :] = jnp.where(admissible, jnp.where(sel, 1.0, 0.0), 0.0)

    mask = sel_s[0:sc, :] > 0.5

    lane4_b = (lax.broadcasted_iota(jnp.int32, (1, hpg * QLANES), 1) % QLANES) // tq
    row4_b = (lax.broadcasted_iota(jnp.int32, (hpg * QLANES, 1), 0) % QLANES) // tq
    for j in range(A_KV):
        q4 = jnp.concatenate([q_ref[:, (j * hpg + g) * A_DH:(j * hpg + g + 1) * A_DH] for g in range(hpg)], axis=0)
        if nb > 1:
            q4f = q4.astype(F32)
            q4 = jnp.concatenate([jnp.where(row4_b == bb, q4f, 0.0).astype(BF16) for bb in range(nb)], axis=1)
        st = _dot_nt(kcat_s[j, 0:sc, :], q4)
        ps, ls = [], []
        for g in range(hpg):
            s = jnp.where(mask, st[:, g * QLANES:(g + 1) * QLANES], NEG)
            mx = _colmax(s)
            p = jnp.exp2(s - mx)
            ls.append(_colsum(p))
            ps.append(p.astype(BF16))
        p4 = jnp.concatenate(ps, axis=1)
        l4 = jnp.concatenate(ls, axis=1)
        o4 = None
        for bb in range(nb):
            ob = _dot(vt_s[j, bb, :, 0:sc], p4)
            o4 = ob if o4 is None else jnp.where(lane4_b == bb, ob, o4)
        o4 = o4 / l4
        for g in range(hpg):
            h = j * hpg + g
            out_ref[:, h * A_DH:(h + 1) * A_DH] = o4[:, g * QLANES:(g + 1) * QLANES].T.astype(out_ref.dtype)


def _dsa_kernel(*refs, tq, nb, s_real, s_pad, s_past, seq, base_pos, topk, classes, steps_per_class):
    if s_past:
        (q_ref, qi_ref, sm_ref, kn_ref, vn_ref, kin_ref, pk_ref, pv_ref, pki_ref,
         out_ref, kcat_s, vt_s, ki_s, sel_s, vtmp_s) = refs
    else:
        (q_ref, qi_ref, sm_ref, kn_ref, vn_ref, kin_ref, out_ref, kcat_s, vt_s, ki_s, sel_s, vtmp_s) = refs
    qb = pl.program_id(1)

    @pl.when(qb == 0)
    def _():
        for bb in range(nb):
            rows = slice(bb * seq, (bb + 1) * seq)
            if s_past:
                ki_s[bb, 0:s_past, :] = pki_ref[bb].astype(BF16)
            ki_s[bb, s_past:s_real, :] = kin_ref[rows, :].astype(BF16)
            if s_pad > s_real:
                ki_s[bb, s_real:s_pad, :] = jnp.zeros((s_pad - s_real, I_DH), BF16)
            for j in range(A_KV):
                cols = slice(j * A_DH, (j + 1) * A_DH)
                lanes = slice(bb * A_DH, (bb + 1) * A_DH)
                if s_past:
                    kcat_s[j, 0:s_past, lanes] = pk_ref[bb, pl.ds(j, s_past, stride=A_KV), :].astype(BF16)
                    vtmp_s[0:s_past, :] = pv_ref[bb, pl.ds(j, s_past, stride=A_KV), :]
                kcat_s[j, s_past:s_real, lanes] = kn_ref[rows, cols].astype(BF16)
                vtmp_s[s_past:s_real, :] = vn_ref[rows, cols].astype(F32)
                if s_pad > s_real:
                    kcat_s[j, s_real:s_pad, lanes] = jnp.zeros((s_pad - s_real, A_DH), BF16)
                    vtmp_s[s_real:s_pad, :] = jnp.zeros((s_pad - s_real, A_DH), F32)
                vt_s[j, bb] = vtmp_s[...].T.astype(BF16)

    body = functools.partial(_dsa_body, q_ref, qi_ref, sm_ref, out_ref, kcat_s, vt_s, ki_s, sel_s,
                             s_real=s_real, nb=nb, tq=tq, base_pos=base_pos, topk=topk, qb=qb)
    if len(classes) == 1:
        body(sc=classes[0])
    else:
        for c, sc in enumerate(classes):
            pl.when(qb // steps_per_class == c)(functools.partial(body, sc=sc))


def dsa(q_r, qi_r, p_small, k_new, v_src, v_colblk, ki_new, past, batch, seq, base_pos):
    m = batch * seq
    tq = min(seq, QLANES)
    nb = QLANES // tq
    nq = seq // tq
    s_past = 0 if past is None else past[2].shape[1]
    s_real = s_past + seq
    s_pad = -(-s_real // LANES) * LANES
    topk = min(TOPK_MAX, s_real // 4)
    assert topk <= s_real and s_pad < 8192 and batch % nb == 0
    n_cls = 8 if (past is None and nb == 1 and nq % 8 == 0 and s_pad % (8 * LANES) == 0
                  and s_pad // 8 >= topk) else 1
    classes = tuple(s_pad * (c + 1) // n_cls for c in range(n_cls))
    assert n_cls == 1 or base_pos == 0
    qrow = lambda w: pl.BlockSpec((QLANES, w), lambda b, i: (b * nq + i, 0))
    in_specs = [qrow(1024), qrow(512), qrow(LANES),
                pl.BlockSpec((nb * seq, 256), lambda b, i: (b, 0)),
                pl.BlockSpec((nb * seq, 256), lambda b, i: (b, v_colblk)),
                pl.BlockSpec((nb * seq, I_DH), lambda b, i: (b, 0))]
    args = [q_r, qi_r, p_small, k_new, v_src, ki_new]
    if past is not None:
        in_specs += [pl.BlockSpec((nb, s_past * A_KV, A_DH), lambda b, i: (b, 0, 0)),
                     pl.BlockSpec((nb, s_past * A_KV, A_DH), lambda b, i: (b, 0, 0)),
                     pl.BlockSpec((nb, s_past, I_DH), lambda b, i: (b, 0, 0))]
        args += list(past)
    kern = functools.partial(_dsa_kernel, tq=tq, nb=nb, s_real=s_real, s_pad=s_pad, s_past=s_past, seq=seq,
                             base_pos=base_pos, topk=topk, classes=classes, steps_per_class=nq // n_cls)
    return pl.pallas_call(
        kern,
        out_shape=jax.ShapeDtypeStruct((m, A_HEADS * A_DH), BF16),
        grid=(batch // nb, nq),
        in_specs=in_specs,
        out_specs=pl.BlockSpec((QLANES, 1024), lambda b, i: (b * nq + i, 0)),
        scratch_shapes=[pltpu.VMEM((A_KV, s_pad, nb * A_DH), BF16),
                        pltpu.VMEM((A_KV, nb, A_DH, s_pad), BF16),
                        pltpu.VMEM((nb, s_pad, I_DH), BF16),
                        pltpu.VMEM((s_pad, QLANES), F32),
                        pltpu.VMEM((s_pad, A_DH), F32)],
        compiler_params=_cparams(("arbitrary", "arbitrary")),
        name="dsa",
    )(*args)


def _mem_attn_kernel(q_ref, k_ref, v_ref, o_ref, *, rows_by_head):
    for h in range(C_HEADS):
        sl = slice(h * C_DH, (h + 1) * C_DH)
        if rows_by_head:
            parts = C_DH // LANES
            step = C_HEADS * parts
            kh = jnp.concatenate([k_ref[pl.ds(c * C_HEADS + h, N_MEM, stride=step), :] for c in range(parts)], axis=1)
            vh = jnp.concatenate([v_ref[pl.ds(c * C_HEADS + h, N_MEM, stride=step), :] for c in range(parts)], axis=1)
        else:
            kh, vh = k_ref[:, sl], v_ref[:, sl]
        s = _dot_nt(q_ref[:, sl].astype(BF16), kh.astype(BF16)) * (C_DH ** -0.5)
        mx = jnp.max(s, axis=-1, keepdims=True)
        p = jnp.exp(s - mx)
        l = jnp.sum(p, axis=-1, keepdims=True)
        o_ref[:, sl] = (_dot(p.astype(BF16), vh.astype(BF16)) / l).astype(o_ref.dtype)


def _mem_rows(cache):
    b = cache.shape[0]
    parts = C_DH // LANES
    x = cache.reshape(b, N_MEM, C_HEADS, parts, LANES).transpose(0, 1, 3, 2, 4)
    return x.reshape(b * N_MEM * parts * C_HEADS, LANES)


def mem_attn(p_main, mk, mk_colblk, mv, mv_colblk, batch, seq, tq):
    m = batch * seq
    nq = seq // tq
    w = C_HEADS * C_DH
    rows_by_head = mk_colblk is None
    if rows_by_head:
        kv_specs = [pl.BlockSpec((N_MEM * w // LANES, LANES), lambda b, i: (b, 0))] * 2
    else:
        kv_specs = [pl.BlockSpec((N_MEM, w), lambda b, i: (b, mk_colblk)),
                    pl.BlockSpec((N_MEM, w), lambda b, i: (b, mv_colblk))]
    return pl.pallas_call(
        functools.partial(_mem_attn_kernel, rows_by_head=rows_by_head),
        out_shape=jax.ShapeDtypeStruct((m, w), BF16),
        grid=(batch, nq),
        in_specs=[pl.BlockSpec((tq, w), lambda b, i: (b * nq + i, COL["cq"] // w))] + kv_specs,
        out_specs=pl.BlockSpec((tq, w), lambda b, i: (b * nq + i, 0)),
        compiler_params=_cparams(("arbitrary", "arbitrary")),
        name="mem_attn",
    )(p_main, mk, mv)


def _merge_kernel(a_ref, b_ref, c_ref, ga_ref, gb_ref, gc_ref, w_ref, o_ref):
    acc = None
    for br_ref, g_ref, n in ((a_ref, ga_ref, 0), (b_ref, gb_ref, 1), (c_ref, gc_ref, 2)):
        proj = _dot(br_ref[...], w_ref[n])
        term = proj / (1.0 + jnp.exp(-g_ref[...].astype(F32)))
        acc = term if acc is None else acc + term
    o_ref[...] = acc.astype(o_ref.dtype)


def merge(br_a, br_b, br_c, p_main, w_branch, tm, tn):
    m = br_a.shape[0]
    nj = D_MODEL // tn
    brs = pl.BlockSpec((tm, BRANCH_W), lambda i, j: (i, 0))
    gate = lambda n: pl.BlockSpec((tm, tn), lambda i, j: (i, (COL["gt"] + n * D_MODEL) // tn + j))
    return pl.pallas_call(
        _merge_kernel,
        out_shape=jax.ShapeDtypeStruct((m, D_MODEL), BF16),
        grid=(m // tm, nj),
        in_specs=[brs, brs, brs, gate(0), gate(1), gate(2),
                  pl.BlockSpec((N_BRANCH, BRANCH_W, tn), lambda i, j: (0, 0, j))],
        out_specs=pl.BlockSpec((tm, tn), lambda i, j: (i, j)),
        compiler_params=_cparams(("arbitrary", "arbitrary")),
        name="merge",
    )(br_a, br_b, br_c, p_main, p_main, p_main, w_branch)


def _ffn_kernel(h_ref, g_ref, wg_ref, wv_ref, cwb_ref, pg_ref, pv_ref, wd_ref, gf_ref,
                y_ref, tg_ref, tv_ref, xn_ref, ug_s, uv_s, cg_s, cv_s, *, tm, rb, seg, blocks_per_seq):
    i = pl.program_id(0)
    j = pl.program_id(1)
    hdr = SUBLANES
    sub = min(seg, rb)

    @pl.when(j == 0)
    def _():
        x = h_ref[...]
        ms = jnp.mean(x * x, axis=-1, keepdims=True)
        xn_ref[...] = (x * lax.rsqrt(ms + EPS) * g_ref[...]).astype(BF16)
        y_ref[...] = jnp.zeros_like(y_ref)

    if blocks_per_seq > 1:
        @pl.when((i % blocks_per_seq) == 0)
        def _():
            cg_s[j] = pg_ref[0]
            cv_s[j] = pv_ref[0]

    tn = wg_ref.shape[1]
    cwb_g = cwb_ref[:, pl.ds(pl.multiple_of(j * tn, tn), tn)]
    cwb_v = cwb_ref[:, pl.ds(pl.multiple_of(D_FF + j * tn, tn), tn)]

    def conv(u, prev, cwb, u_s):
        u_s[hdr - 2:hdr, :] = prev
        u_s[hdr:hdr + sub, :] = u
        return (cwb[3:4, :] + u_s[hdr - 2:hdr - 2 + sub, :] * cwb[0:1, :]
                + u_s[hdr - 1:hdr - 1 + sub, :] * cwb[1:2, :] + u * cwb[2:3, :])

    tail_g = tail_v = None
    for r in range(tm // rb):
        xn = xn_ref[r * rb:(r + 1) * rb, :]
        ug = _dot(xn, wg_ref[...])
        uv = _dot(xn, wv_ref[...])
        acts = []
        for s in range(rb // sub):
            row0 = r * rb + s * sub
            sq = row0 // seg
            rows = slice(s * sub, (s + 1) * sub)
            ugs, uvs = ug[rows, :], uv[rows, :]
            if row0 % seg == 0:
                prev_g = cg_s[j] if blocks_per_seq > 1 else pg_ref[sq]
                prev_v = cv_s[j] if blocks_per_seq > 1 else pv_ref[sq]
            else:
                prev_g, prev_v = tail_g, tail_v
            cg = conv(ugs, prev_g, cwb_g, ug_s.at[r])
            cv = conv(uvs, prev_v, cwb_v, uv_s.at[r])
            acts.append((cg / (1.0 + jnp.exp(-cg)) * cv).astype(BF16))
            tail_g, tail_v = ugs[sub - 2:sub, :], uvs[sub - 2:sub, :]
            if (row0 + sub) % seg == 0:
                tg_ref[sq] = tail_g
                tv_ref[sq] = tail_v
                if blocks_per_seq > 1:
                    cg_s[j] = tail_g
                    cv_s[j] = tail_v
        act = acts[0] if len(acts) == 1 else jnp.concatenate(acts, axis=0)
        cw = 512
        for c in range(D_MODEL // cw):
            cols = slice(c * cw, (c + 1) * cw)
            y_ref[r * rb:(r + 1) * rb, cols] += _dot(act, wd_ref[:, cols])

    @pl.when(j == pl.num_programs(1) - 1)
    def _():
        z = h_ref[...] + y_ref[...]
        ms = jnp.mean(z * z, axis=-1, keepdims=True)
        y_ref[...] = z * lax.rsqrt(ms + EPS) * gf_ref[...]


def ffn(h, g_ffn, w_up, conv_w, conv_b, conv_prev, w_down, g_final, seq, tm, tn):
    m = h.shape[0]
    seg = min(tm, seq)
    nseg = tm // seg
    bps = seq // seg
    blocks_per_seq = bps if nseg == 1 else 1
    nj = D_FF // tn
    prev_spec = lambda off: pl.BlockSpec((nseg, CONV_W - 1, tn),
                                         lambda i, j: ((i // blocks_per_seq) if nseg == 1 else i, 0, off + j))
    tail_spec = pl.BlockSpec((nseg, CONV_W - 1, tn), lambda i, j: (i, 0, j))
    rb = min(tm, 512)
    sub = min(seg, rb)
    kern = functools.partial(_ffn_kernel, tm=tm, rb=rb, seg=seg, blocks_per_seq=blocks_per_seq)
    return pl.pallas_call(
        kern,
        out_shape=(jax.ShapeDtypeStruct((m, D_MODEL), F32),
                   jax.ShapeDtypeStruct((m // seg, CONV_W - 1, D_FF), F32),
                   jax.ShapeDtypeStruct((m // seg, CONV_W - 1, D_FF), F32)),
        grid=(m // tm, nj),
        in_specs=[pl.BlockSpec((tm, D_MODEL), lambda i, j: (i, 0), pipeline_mode=pl.Buffered(1)),
                  pl.BlockSpec((1, D_MODEL), lambda i, j: (0, 0)),
                  pl.BlockSpec((D_MODEL, tn), lambda i, j: (0, j)),
                  pl.BlockSpec((D_MODEL, tn), lambda i, j: (0, nj + j)),
                  pl.BlockSpec((CONV_W + 1, 2 * D_FF), lambda i, j: (0, 0)),
                  prev_spec(0), prev_spec(nj),
                  pl.BlockSpec((tn, D_MODEL), lambda i, j: (j, 0)),
                  pl.BlockSpec((1, D_MODEL), lambda i, j: (0, 0))],
        out_specs=(pl.BlockSpec((tm, D_MODEL), lambda i, j: (i, 0), pipeline_mode=pl.Buffered(1)),
                   tail_spec, tail_spec),
        scratch_shapes=[pltpu.VMEM((tm, D_MODEL), BF16),
                        pltpu.VMEM((tm // rb, SUBLANES + sub, tn), F32),
                        pltpu.VMEM((tm // rb, SUBLANES + sub, tn), F32),
                        pltpu.VMEM((nj, CONV_W - 1, tn), F32), pltpu.VMEM((nj, CONV_W - 1, tn), F32)],
        compiler_params=pltpu.CompilerParams(dimension_semantics=("arbitrary", "arbitrary"),
                                             vmem_limit_bytes=FFN_VMEM_LIMIT),
        name="ffn",
    )(h, g_ffn.reshape(1, D_MODEL), w_up, w_up,
      jnp.concatenate([conv_w, conv_b.reshape(1, 2 * D_FF)], axis=0), conv_prev, conv_prev,
      w_down, g_final.reshape(1, D_MODEL))


def _group(x, pos, past, mstate, mem_k, mem_kblk, mem_v, mem_vblk, conv_prev, wts, base_pos):
    (g_mix, w_main, w_small, b_ig, b_fg, w_branch, w_out, g_ffn, w_up, conv_w, conv_b, w_down, g_final) = wts
    batch, seq, _ = x.shape
    m = batch * seq
    x2 = x.reshape(m, D_MODEL)

    p_main = norm_matmul(x2, g_mix, w_main, 1024, 1024, "in_proj", out_dtype=BF16)
    p_small = norm_matmul(x2, g_mix, w_small, 1024, LANES, "in_proj_small")

    q_r, qi_r, k_new, ki_new = rope(p_main, p_small, pos, seq, 512)
    v_new = p_main[:, COL["av"]:COL["av"] + A_KV * A_DH].astype(F32)

    c0, n0, m0 = mstate
    br_a, c_new, n_new, m_new = mlstm(p_main, p_small, b_ig, b_fg, c0, n0,
                                      jnp.broadcast_to(m0[:, :, None], (batch, M_HEADS, LANES)), batch, seq)
    br_b = dsa(q_r, qi_r, p_small, k_new, p_main, COL["av"] // 256, ki_new, past, batch, seq, base_pos)
    br_c = mem_attn(p_main, mem_k, mem_kblk, mem_v, mem_vblk, batch, seq, min(seq, 512))

    merged = merge(br_a, br_b, br_c, p_main, w_branch, 1024, 512)
    h = matmul_res(merged, w_out, x2, 1024, 1024, "out_proj")

    y, tail_g, tail_v = ffn(h, g_ffn, w_up, conv_w, conv_b, conv_prev, w_down, g_final, seq, 1024, 512)

    nblk = tail_g.shape[0] // batch
    conv_new = jnp.concatenate([tail_g.reshape(batch, nblk, CONV_W - 1, D_FF)[:, -1],
                                tail_v.reshape(batch, nblk, CONV_W - 1, D_FF)[:, -1]], axis=-1)
    outs = (y.reshape(batch, seq, D_MODEL),
            k_new.reshape(batch, seq, A_KV, A_DH), v_new.reshape(batch, seq, A_KV, A_DH),
            ki_new.reshape(batch, seq, I_DH), c_new, n_new, m_new[:, :, 0], conv_new)
    return outs


def kernel(x_prompt, x_sample, mem_prompt, cache_dsa_k, cache_dsa_v, cache_dsa_kidx, state_mlstm_C, state_mlstm_n, state_mlstm_m, state_ffn_conv, cache_mem_k, cache_mem_v, g_mix, w_in, b_igate, b_fgate, w_branch, w_out, g_mem, w_mem_kv, g_ffn, w_up, conv_w, conv_b, w_down, g_final):
    depth = w_in.shape[0]
    assert depth == 1
    B, T = x_prompt.shape[:2]
    DB, DT = x_sample.shape[:2]
    P = cache_dsa_k.shape[2]
    l = 0

    w = w_in[l]
    segs = {}
    off = 0
    for name, width in zip(ORIG_NAMES, ORIG_SPLITS):
        segs[name] = w[:, off:off + width]
        off += width
    w_main = jnp.concatenate([segs[n] for n in MAIN_ORDER], axis=1).astype(BF16)
    small_w = sum(ORIG_SPLITS[ORIG_NAMES.index(n)] for n in SMALL_ORDER)
    w_small = jnp.concatenate([segs[n] for n in SMALL_ORDER] + [jnp.zeros((D_MODEL, LANES - small_w), F32)],
                              axis=1).astype(BF16)
    wts = (g_mix[l], w_main, w_small, b_igate[l], b_fgate[l], w_branch[l].astype(BF16), w_out[l].astype(BF16),
           g_ffn[l], w_up[l].astype(BF16), conv_w[l], conv_b[l], w_down[l].astype(BF16), g_final)

    mkv = norm_matmul(mem_prompt.reshape(B * N_MEM, D_MODEL), g_mem[l], w_mem_kv[l].astype(BF16), 512, 1024, "mem_kv")
    zero_state = (jnp.zeros((B, M_HEADS, M_DV, M_DK), F32), jnp.zeros((B, M_HEADS, M_DK), F32),
                  jnp.zeros((B, M_HEADS), F32))
    zero_conv = jnp.zeros((B, CONV_W - 1, 2 * D_FF), F32)
    (y_p, k_p, v_p, ki_p, c_p, n_p, m_p, conv_p) = _group(
        x_prompt, jnp.arange(T, dtype=jnp.int32), None, zero_state, mkv, 0, mkv, 1, zero_conv, wts, 0)
    hw = C_HEADS * C_DH
    mk_p = mkv[:, :hw].reshape(B, N_MEM, C_HEADS, C_DH)
    mv_p = mkv[:, hw:].reshape(B, N_MEM, C_HEADS, C_DH)

    past = (cache_dsa_k[l].reshape(DB, P * A_KV, A_DH), cache_dsa_v[l].reshape(DB, P * A_KV, A_DH),
            cache_dsa_kidx[l])
    mstate = (state_mlstm_C[l], state_mlstm_n[l], state_mlstm_m[l])
    (y_s, k_s, v_s, ki_s, c_s, n_s, m_s, conv_s) = _group(
        x_sample, P + jnp.arange(DT, dtype=jnp.int32), past, mstate,
        _mem_rows(cache_mem_k[l]), None, _mem_rows(cache_mem_v[l]), None,
        state_ffn_conv[l], wts, P)

    st = lambda a: a[None]
    return (y_p, y_s,
            st(k_p), st(v_p), st(ki_p), st(c_p), st(n_p), st(m_p), st(conv_p), st(mk_p), st(mv_p),
            st(k_s), st(v_s), st(ki_s), st(c_s), st(n_s), st(m_s), st(conv_s))
```
